```python
import math
import jax
import jax.numpy as jnp
from jax import lax
import numpy as np

D_MODEL = 4096
BATCH = 2
SEQ = 8192
DEPTH = 4

GRID_W = 64
CTX_LEN = 256
HEAD_DIM = 128
ROPE_THETA = 10000.0
NORM_EPS = 1e-6
Q_BLOCK = 128
N_MOD = 6
N_PAIRS = DEPTH // 2

A_HEADS = D_MODEL // (2 * HEAD_DIM)
A_KV_HEADS = A_HEADS // 4
A_GROUP = A_HEADS // A_KV_HEADS
A_Q = A_HEADS * HEAD_DIM
A_KV = A_KV_HEADS * HEAD_DIM

B_D_INNER = D_MODEL // 2
B_HEAD_DIM = 64
B_HEADS = B_D_INNER // B_HEAD_DIM
B_GROUPS = 4
B_HPG = B_HEADS // B_GROUPS
B_STATE = 128
B_GN = B_GROUPS * B_STATE
B_CONV_CH = B_D_INNER + 2 * B_GN
B_CONV_W = 5
B_CHUNK = 128
DT_MIN = 1e-3
DT_MAX = 1e-1

C_HEADS = D_MODEL // HEAD_DIM
C_KV_HEADS = C_HEADS // 4
C_GROUP = C_HEADS // C_KV_HEADS
C_Q = C_HEADS * HEAD_DIM
C_KV = C_KV_HEADS * HEAD_DIM
C_WINDOW = 128

N_EXPERTS = 32
TOP_K = 4
F_EXPERT = 192
SWIGLU_LIMIT = 7.0
SWIGLU_ALPHA = 1.702

EVEN_SPLITS = (A_Q, A_KV, A_KV, B_D_INNER, B_CONV_CH, 2 * B_HEADS)
EVEN_IN = sum(EVEN_SPLITS)
EVEN_MIX = A_Q + B_D_INNER
ODD_SPLITS = (C_Q, C_KV, C_KV)
ODD_IN = sum(ODD_SPLITS)

kernel_name = "hybrid_dit_gqa_ssd_swa_moe"


def rms_norm(x, g):
    xf = x.astype(jnp.float32)
    y = xf * lax.rsqrt(jnp.mean(xf * xf, axis=-1, keepdims=True) + NORM_EPS)
    return (y * g.astype(jnp.float32)).astype(x.dtype)


def modulate(h, shift, scale):
    return h * (1 + scale) + shift


def ada_mod(cvec, w, b, n_chunks):
    width = n_chunks * D_MODEL
    m = jax.nn.silu(cvec) @ w[:, :width] + b[:width]
    return jnp.split(m, n_chunks, axis=-1)


def split_cols(t, sizes):
    return jnp.split(t, [int(v) for v in np.cumsum(sizes)[:-1]], axis=-1)


def to_heads(t, n_heads):
    return t.reshape(t.shape[0], t.shape[1], n_heads, -1)


def axial_rope_tables(n_tokens):
    rows = n_tokens // GRID_W
    row = jnp.repeat(jnp.arange(rows, dtype=jnp.float32), GRID_W)
    col = jnp.tile(jnp.arange(GRID_W, dtype=jnp.float32), rows)
    n_freq = HEAD_DIM // 4
    inv_freq = ROPE_THETA ** (-jnp.arange(n_freq, dtype=jnp.float32) / n_freq)
    ang = jnp.concatenate([row[:, None] * inv_freq, col[:, None] * inv_freq], axis=-1)
    return jnp.cos(ang), jnp.sin(ang)


def apply_axial_rope(x, cos, sin):
    b, n, h, d = x.shape
    xa = x.astype(jnp.float32).reshape(b, n, h, 2, 2, d // 4)
    cs = cos.reshape(n, 1, 2, d // 4)
    sn = sin.reshape(n, 1, 2, d // 4)
    x1, x2 = xa[..., 0, :], xa[..., 1, :]
    out = jnp.stack([x1 * cs - x2 * sn, x2 * cs + x1 * sn], axis=-2)
    return out.reshape(b, n, h, d).astype(x.dtype)


def attend(q, k, v, mask=None, sink=None):
    s = jnp.einsum("bqkgd,btkd->bkgqt", q, k, preferred_element_type=jnp.float32) * (q.shape[-1] ** -0.5)
    if mask is not None:
        s = jnp.where(mask, s, -jnp.inf)
    if sink is None:
        p = jax.nn.softmax(s, axis=-1)
    else:
        sk = jnp.broadcast_to(sink.astype(jnp.float32)[None, :, :, None, None], s.shape[:-1] + (1,))
        p = jax.nn.softmax(jnp.concatenate([s, sk], axis=-1), axis=-1)[..., :-1]
    o = jnp.einsum("bkgqt,btkd->bqkgd", p.astype(v.dtype), v)
    return o.reshape(o.shape[0], o.shape[1], -1)


def depthwise_conv(x, w, b):
    pad = B_CONV_W // 2
    y = lax.conv_general_dilated(x, w[:, None, :].astype(x.dtype), window_strides=(1,), padding=[(pad, pad)],
                                 dimension_numbers=("NWC", "WIO", "NWC"), feature_group_count=x.shape[-1])
    return y + b


def ssd_scan(xh, dt, a_coef, bm, cm, h0):
    f32 = jnp.float32
    b, l, g, j, p = xh.shape
    n = bm.shape[-1]
    nc = l // B_CHUNK
    ld = (dt.astype(f32) * a_coef.astype(f32)).reshape(b, nc, B_CHUNK, g, j)
    xdt = (xh.astype(f32) * dt.astype(f32)[..., None]).reshape(b, nc, B_CHUNK, g, j, p)
    bmc = bm.astype(f32).reshape(b, nc, B_CHUNK, g, n)
    cmc = cm.astype(f32).reshape(b, nc, B_CHUNK, g, n)
    cum = jnp.cumsum(ld, axis=2)
    lower = jnp.tril(jnp.ones((B_CHUNK, B_CHUNK), dtype=bool))[None, None, :, :, None, None]
    seg = jnp.exp(jnp.where(lower, cum[:, :, :, None] - cum[:, :, None, :], -jnp.inf))
    cb = jnp.einsum("bclgn,bcsgn->bclsg", cmc, bmc)
    y_diag = jnp.einsum("bclsg,bclsgj,bcsgjp->bclgjp", cb, seg, xdt)
    states = jnp.einsum("bcsgn,bcsgj,bcsgjp->bcgjpn", bmc, jnp.exp(cum[:, :, -1:] - cum), xdt)
    chunk_decay = jnp.exp(cum[:, :, -1])

    def step(h, inp):
        st, dec = inp
        return h * dec[..., None, None] + st, h

    h_last, h_prev = lax.scan(step, h0.astype(f32), (jnp.moveaxis(states, 1, 0), jnp.moveaxis(chunk_decay, 1, 0)))
    h_prev = jnp.moveaxis(h_prev, 0, 1)
    y_off = jnp.einsum("bclgn,bcgjpn,bclgj->bclgjp", cmc, h_prev, jnp.exp(cum))
    return (y_diag + y_off).reshape(b, l, g, j, p), h_last


def ssm_inputs(xbc, dt_raw, conv_w, conv_b, dt_bias):
    b, l, _ = xbc.shape
    xbc = jax.nn.silu(depthwise_conv(xbc, conv_w, conv_b))
    xs, bm, cm = split_cols(xbc, (B_D_INNER, B_GN, B_GN))
    xs = xs.reshape(b, l, B_GROUPS, B_HPG, B_HEAD_DIM)
    bm = bm.reshape(b, l, B_GROUPS, B_STATE)
    cm = cm.reshape(b, l, B_GROUPS, B_STATE)
    dt = jax.nn.softplus((dt_raw.reshape(b, l, 2, B_GROUPS, B_HPG)
                          + dt_bias.reshape(2, B_GROUPS, B_HPG)).astype(jnp.float32))
    return xs, bm, cm, dt


def gated_norm(y, z, g):
    b, l = z.shape[:2]
    yz = y.reshape(b, l, B_GROUPS, -1).astype(z.dtype) * jax.nn.silu(z.reshape(b, l, B_GROUPS, -1))
    return rms_norm(yz, g.reshape(B_GROUPS, -1)).reshape(b, l, B_D_INNER)


def maybe_flip(t, direction):
    return jnp.flip(t, axis=1) if direction == 1 else t


def ssd_mixer(zl, xbcl, dtl, zc, xbcc, dtc, conv_w, conv_b, a_log, dt_bias, d_skip, ssm_g):
    xl, bl, cl, dl = ssm_inputs(xbcl, dtl, conv_w, conv_b, dt_bias)
    xc, bc, cc, dc = ssm_inputs(xbcc, dtc, conv_w, conv_b, dt_bias)
    a_coef = -jnp.exp(a_log.astype(jnp.float32)).reshape(2, B_GROUPS, B_HPG)
    skip = d_skip.reshape(2, B_GROUPS, B_HPG, 1)
    h0 = jnp.zeros((zl.shape[0], B_GROUPS, B_HPG, B_HEAD_DIM, B_STATE), jnp.float32)
    ys_l, ys_c = [], []
    for d in range(2):
        yc, hc_last = ssd_scan(maybe_flip(xc, d), maybe_flip(dc[:, :, d], d), a_coef[d],
                               maybe_flip(bc, d), maybe_flip(cc, d), h0)
        yl, _ = ssd_scan(maybe_flip(xl, d), maybe_flip(dl[:, :, d], d), a_coef[d],
                         maybe_flip(bl, d), maybe_flip(cl, d), hc_last)
        ys_c.append(maybe_flip(yc, d) + skip[d] * xc)
        ys_l.append(maybe_flip(yl, d) + skip[d] * xl)
    return gated_norm(ys_l[0] + ys_l[1], zl, ssm_g), gated_norm(ys_c[0] + ys_c[1], zc, ssm_g)


def even_mixer(hl, hc, cos, sin, w_in, w_out, q_g, k_g, conv_w, conv_b, a_log, dt_bias, d_skip, ssm_g):
    b, s, _ = hl.shape
    n_ctx = hc.shape[1]
    ql, kl, vl, zl, xbcl, dtl = split_cols(hl @ w_in, EVEN_SPLITS)
    qc, kc, vc, zc, xbcc, dtc = split_cols(hc @ w_in, EVEN_SPLITS)

    ql = apply_axial_rope(rms_norm(to_heads(ql, A_HEADS), q_g), cos, sin)
    kl = apply_axial_rope(rms_norm(to_heads(kl, A_KV_HEADS), k_g), cos, sin)
    qc = rms_norm(to_heads(qc, A_HEADS), q_g)
    kc = rms_norm(to_heads(kc, A_KV_HEADS), k_g)
    vc = to_heads(vc, A_KV_HEADS)
    keys = jnp.concatenate([kc, kl], axis=1)
    vals = jnp.concatenate([vc, to_heads(vl, A_KV_HEADS)], axis=1)
    n_blk = s // Q_BLOCK
    q_blocks = jnp.moveaxis(ql.reshape(b, n_blk, Q_BLOCK, A_KV_HEADS, A_GROUP, HEAD_DIM), 1, 0)
    attn_l = lax.map(lambda q_blk: attend(q_blk, keys, vals), q_blocks)
    attn_l = jnp.moveaxis(attn_l, 0, 1).reshape(b, s, A_Q)
    attn_c = attend(qc.reshape(b, n_ctx, A_KV_HEADS, A_GROUP, HEAD_DIM), kc, vc)

    ssm_l, ssm_c = ssd_mixer(zl, xbcl, dtl, zc, xbcc, dtc, conv_w, conv_b, a_log, dt_bias, d_skip, ssm_g)

    out_l = jnp.concatenate([attn_l, ssm_l], axis=-1) @ w_out
    out_c = jnp.concatenate([attn_c, ssm_c], axis=-1) @ w_out
    return out_l, out_c


def odd_mixer(hl, hc, cos, sin, w_in, w_out, sinks, ctx_out):
    b, s, _ = hl.shape
    n_ctx = hc.shape[1]
    ql, kl, vl = split_cols(hl @ w_in, ODD_SPLITS)
    ql = apply_axial_rope(to_heads(ql, C_HEADS), cos, sin)
    kl = apply_axial_rope(to_heads(kl, C_KV_HEADS), cos, sin)
    vl = to_heads(vl, C_KV_HEADS)
    if ctx_out:
        qc, kc, vc = split_cols(hc @ w_in, ODD_SPLITS)
    else:
        kc, vc = split_cols(hc @ w_in[:, C_Q:], ODD_SPLITS[1:])
    kc, vc = to_heads(kc, C_KV_HEADS), to_heads(vc, C_KV_HEADS)
    sink = sinks.reshape(C_KV_HEADS, C_GROUP)
    band = Q_BLOCK + 2 * C_WINDOW
    pad = ((0, 0), (C_WINDOW, C_WINDOW), (0, 0), (0, 0))
    k_pad, v_pad = jnp.pad(kl, pad), jnp.pad(vl, pad)
    n_blk = s // Q_BLOCK
    q_blocks = jnp.moveaxis(ql.reshape(b, n_blk, Q_BLOCK, C_KV_HEADS, C_GROUP, HEAD_DIM), 1, 0)
    ctx_visible = jnp.ones((Q_BLOCK, n_ctx), dtype=bool)

    def one_block(args):
        q_blk, blk = args
        start = blk * Q_BLOCK
        k_win = lax.dynamic_slice_in_dim(k_pad, start, band, axis=1)
        v_win = lax.dynamic_slice_in_dim(v_pad, start, band, axis=1)
        key_pos = start - C_WINDOW + jnp.arange(band)
        q_pos = start + jnp.arange(Q_BLOCK)
        near = ((jnp.abs(key_pos[None, :] - q_pos[:, None]) <= C_WINDOW)
                & (key_pos[None, :] >= 0) & (key_pos[None, :] < s))
        mask = jnp.concatenate([ctx_visible, near], axis=1)
        return attend(q_blk, jnp.concatenate([kc, k_win], axis=1), jnp.concatenate([vc, v_win], axis=1),
                      mask=mask, sink=sink)

    attn_l = jnp.moveaxis(lax.map(one_block, (q_blocks, jnp.arange(n_blk))), 0, 1).reshape(b, s, C_Q)
    out_l = attn_l @ w_out
    if not ctx_out:
        return out_l, None
    attn_c = attend(qc.reshape(b, n_ctx, C_KV_HEADS, C_GROUP, HEAD_DIM), kc, vc, sink=sink)
    return out_l, attn_c @ w_out


def moe_ffn(h, router_w, router_b, w_gu, b_gu, w_down, b_down):
    logits = (h @ router_w + router_b).astype(jnp.float32)
    top_val, top_idx = lax.top_k(logits, TOP_K)
    top_w = jax.nn.softmax(top_val, axis=-1)
    gates = jnp.sum(jax.nn.one_hot(top_idx, N_EXPERTS, dtype=jnp.float32) * top_w[..., None], axis=-2)
    gates = gates.astype(h.dtype)
    gu = jnp.einsum("bnd,edf->bnef", h, w_gu) + b_gu
    glu = jnp.minimum(gu[..., 0::2], SWIGLU_LIMIT)
    lin = jnp.clip(gu[..., 1::2], -SWIGLU_LIMIT, SWIGLU_LIMIT)
    act = glu * jax.nn.sigmoid(SWIGLU_ALPHA * glu) * (lin + 1)
    return jnp.einsum("bnef,efd->bnd", act * gates[..., None], w_down) + gates @ b_down


def setup_inputs(seed: int = 0) -> dict:
    key = jax.random.key(seed)
    keys = jax.random.split(key, 32)
    counter = iter(range(32))

    def nk():
        return keys[next(counter)]

    def nrm(shape, scale):
        return scale * jax.random.normal(nk(), shape, jnp.float32)

    def gain(shape):
        return 1.0 + nrm(shape, 0.02)

    a_log = jnp.log(jax.random.uniform(nk(), (N_PAIRS, 2, B_HEADS), jnp.float32, 1.0, 16.0))
    dt = jnp.exp(jax.random.uniform(nk(), (N_PAIRS, 2, B_HEADS), jnp.float32,
                                    math.log(DT_MIN), math.log(DT_MAX)))
    return {
        "x": nrm((BATCH, SEQ, D_MODEL), 1.0),
        "c": nrm((BATCH, D_MODEL), 1.0),
        "ctx": nrm((BATCH, CTX_LEN, D_MODEL), 1.0),
        "c_ctx": nrm((D_MODEL,), 1.0),
        "ada_w": nrm((DEPTH, D_MODEL, N_MOD * D_MODEL), 0.5 * D_MODEL ** -0.5),
        "ada_b": nrm((DEPTH, N_MOD * D_MODEL), 0.01),
        "norm_mix_g": gain((DEPTH, D_MODEL)),
        "norm_ffn_g": gain((DEPTH, D_MODEL)),
        "router_w": nrm((DEPTH, D_MODEL, N_EXPERTS), D_MODEL ** -0.5),
        "router_b": nrm((DEPTH, N_EXPERTS), 0.01),
        "exp_w_gu": nrm((DEPTH, N_EXPERTS, D_MODEL, 2 * F_EXPERT), D_MODEL ** -0.5),
        "exp_b_gu": nrm((DEPTH, N_EXPERTS, 2 * F_EXPERT), 0.01),
        "exp_w_down": nrm((DEPTH, N_EXPERTS, F_EXPERT, D_MODEL), F_EXPERT ** -0.5),
        "exp_b_down": nrm((DEPTH, N_EXPERTS, D_MODEL), 0.01),
        "ev_w_in": nrm((N_PAIRS, D_MODEL, EVEN_IN), D_MODEL ** -0.5),
        "ev_w_out": nrm((N_PAIRS, EVEN_MIX, D_MODEL), EVEN_MIX ** -0.5),
        "ev_q_g": gain((N_PAIRS, HEAD_DIM)),
        "ev_k_g": gain((N_PAIRS, HEAD_DIM)),
        "ev_conv_w": nrm((N_PAIRS, B_CONV_W, B_CONV_CH), B_CONV_W ** -0.5),
        "ev_conv_b": nrm((N_PAIRS, B_CONV_CH), 0.01),
        "ev_a_log": a_log,
        "ev_dt_bias": dt + jnp.log(-jnp.expm1(-dt)),
        "ev_d_skip": 1.0 + nrm((N_PAIRS, 2, B_HEADS), 0.1),
        "ev_ssm_g": gain((N_PAIRS, B_D_INNER)),
        "od_w_in": nrm((N_PAIRS, D_MODEL, ODD_IN), D_MODEL ** -0.5),
        "od_w_out": nrm((N_PAIRS, C_Q, D_MODEL), C_Q ** -0.5),
        "od_sinks": nrm((N_PAIRS, C_HEADS), 1.0),
        "final_g": gain((D_MODEL,)),
    }


def reference(x, c, ctx, c_ctx, ada_w, ada_b, norm_mix_g, norm_ffn_g, router_w, router_b,
              exp_w_gu, exp_b_gu, exp_w_down, exp_b_down, ev_w_in, ev_w_out, ev_q_g, ev_k_g,
              ev_conv_w, ev_conv_b, ev_a_log, ev_dt_bias, ev_d_skip, ev_ssm_g,
              od_w_in, od_w_out, od_sinks, final_g):
    cos, sin = axial_rope_tables(x.shape[1])
    xl, xc = x, ctx
    for i in range(DEPTH):
        last = i == DEPTH - 1
        p = i // 2
        ml = ada_mod(c[:, None, :], ada_w[i], ada_b[i], N_MOD)
        mc = ada_mod(c_ctx, ada_w[i], ada_b[i], 2 if last else N_MOD)
        hl = modulate(rms_norm(xl, norm_mix_g[i]), ml[0], ml[1])
        hc = modulate(rms_norm(xc, norm_mix_g[i]), mc[0], mc[1])
        if i % 2 == 0:
            ol, oc = even_mixer(hl, hc, cos, sin, ev_w_in[p], ev_w_out[p], ev_q_g[p], ev_k_g[p],
                                ev_conv_w[p], ev_conv_b[p], ev_a_log[p], ev_dt_bias[p], ev_d_skip[p], ev_ssm_g[p])
        else:
            ol, oc = odd_mixer(hl, hc, cos, sin, od_w_in[p], od_w_out[p], od_sinks[p], not last)
        xl = xl + ml[2] * ol
        hl = modulate(rms_norm(xl, norm_ffn_g[i]), ml[3], ml[4])
        moe_args = (router_w[i], router_b[i], exp_w_gu[i], exp_b_gu[i], exp_w_down[i], exp_b_down[i])
        if last:
            xl = xl + ml[5] * moe_ffn(hl, *moe_args)
        else:
            xc = xc + mc[2] * oc
            hc = modulate(rms_norm(xc, norm_ffn_g[i]), mc[3], mc[4])
            n_ctx = xc.shape[1]
            y = moe_ffn(jnp.concatenate([hc, hl], axis=1), *moe_args)
            xc = xc + mc[5] * y[:, :n_ctx]
            xl = xl + ml[5] * y[:, n_ctx:]
    return rms_norm(xl, final_g)
```

```python
import functools
import math

import jax
import jax.numpy as jnp
import numpy as np
from jax import lax
from jax.experimental import pallas as pl
from jax.experimental.pallas import tpu as pltpu

F32 = jnp.float32
BF16 = jnp.bfloat16

D_MODEL = 4096
HEAD_DIM = 128
GRID_W = 64
ROPE_THETA = 10000.0
NORM_EPS = 1e-6
N_MOD = 6
DEPTH = 4

A_HEADS = D_MODEL // (2 * HEAD_DIM)
A_KV_HEADS = A_HEADS // 4
A_Q = A_HEADS * HEAD_DIM
A_KV = A_KV_HEADS * HEAD_DIM

B_D_INNER = D_MODEL // 2
B_HEAD_DIM = 64
B_HEADS = B_D_INNER // B_HEAD_DIM
B_GROUPS = 4
B_HPG = B_HEADS // B_GROUPS
B_STATE = 128
B_GN = B_GROUPS * B_STATE
B_CONV_CH = B_D_INNER + 2 * B_GN
B_CONV_W = 5
B_CHUNK = 128
B_GROUP_CH = B_D_INNER // B_GROUPS

C_HEADS = D_MODEL // HEAD_DIM
C_KV_HEADS = C_HEADS // 4
C_Q = C_HEADS * HEAD_DIM
C_KV = C_KV_HEADS * HEAD_DIM
C_WINDOW = 128

N_EXPERTS = 32
TOP_K = 4
F_EXPERT = 192
SWIGLU_LIMIT = 7.0
SWIGLU_ALPHA = 1.702
EXPERTS_PER_STEP = 2
N_EXPERT_STEPS = N_EXPERTS // EXPERTS_PER_STEP

LANES = 128
ROW_TILE = 512
COL_TILE = 512
ATTN_TQ = 256
ATTN_TK = 512
PREP_TILE = 256
VMEM_LIMIT = 56 * 1024 * 1024

EVEN_MAIN = A_Q + 2 * A_KV + B_D_INNER + B_CONV_CH
EVEN_QK = A_Q + A_KV
ODD_IN = C_Q + 2 * C_KV
ODD_QK = C_Q + C_KV


def _params(sem):
    return pltpu.CompilerParams(dimension_semantics=sem, vmem_limit_bytes=VMEM_LIMIT)


def _sigmoid(v):
    return 1.0 / (1.0 + jnp.exp(-v))


def _softplus(v):
    return jnp.maximum(v, 0.0) + jnp.log(1.0 + jnp.exp(-jnp.abs(v)))


def _ada_kernel(c_ref, w_ref, b_ref, o_ref):
    c = c_ref[...]
    s = (c * _sigmoid(c)).astype(BF16)
    o_ref[0] = jnp.dot(s, w_ref[0].astype(BF16), preferred_element_type=F32) + b_ref[0]


def ada_mods(cvecs, ada_w, ada_b):
    depth, d, n = ada_w.shape
    tn = COL_TILE
    return pl.pallas_call(
        _ada_kernel,
        grid=(depth, n // tn),
        in_specs=[
            pl.BlockSpec((8, d), lambda l, j: (0, 0)),
            pl.BlockSpec((1, d, tn), lambda l, j: (l, 0, j)),
            pl.BlockSpec((1, 1, tn), lambda l, j: (l, 0, j)),
        ],
        out_specs=pl.BlockSpec((1, 8, tn), lambda l, j: (l, 0, j)),
        out_shape=jax.ShapeDtypeStruct((depth, 8, n), F32),
        compiler_params=_params(("parallel", "parallel")),
        name="ada_mods",
    )(cvecs, ada_w, ada_b.reshape(depth, 1, n))


def _norm_mod(x, g, shift, scale):
    ms = jnp.mean(x * x, axis=-1, keepdims=True)
    return (x * lax.rsqrt(ms + NORM_EPS) * g) * (1.0 + scale) + shift


def _inproj_kernel(*refs, has_prev, has_dt):
    it = iter(refs)
    x_ref = next(it)
    if has_prev:
        y_ref = next(it)
        pmod_ref = next(it)
    mod_ref = next(it)
    g_ref = next(it)
    w_ref = next(it)
    if has_dt:
        wdt_ref = next(it)
    o_ref = next(it)
    if has_dt:
        dt_ref = next(it)
    h_scr = next(it)

    @pl.when(pl.program_id(1) == 0)
    def _():
        x = x_ref[...]
        if has_prev:
            x = x + pmod_ref[0, 5:6, :] * y_ref[...].astype(F32)
        h = _norm_mod(x, g_ref[...], mod_ref[0, 0:1, :], mod_ref[0, 1:2, :])
        h_scr[...] = h.astype(BF16)
        if has_dt:
            dt_ref[...] = jnp.dot(h_scr[...], wdt_ref[...], preferred_element_type=F32)

    o_ref[...] = jnp.dot(h_scr[...], w_ref[...], preferred_element_type=F32).astype(o_ref.dtype)


def in_projection(x, prev, mods, g, w, w_dt, mod_of_tile):
    m, d = x.shape
    n = w.shape[1]
    tm, tn = ROW_TILE, COL_TILE
    has_prev = prev is not None
    has_dt = w_dt is not None
    mod_spec = pl.BlockSpec((1, N_MOD, d), lambda i, j: (mod_of_tile(i), 0, 0))
    args = [x]
    specs = [pl.BlockSpec((tm, d), lambda i, j: (i, 0))]
    if has_prev:
        args += [prev[0], prev[1]]
        specs += [pl.BlockSpec((tm, d), lambda i, j: (i, 0)), mod_spec]
    args += [mods, g.reshape(1, d), w]
    specs += [mod_spec, pl.BlockSpec((1, d), lambda i, j: (0, 0)), pl.BlockSpec((d, tn), lambda i, j: (0, j))]
    out_shape = [jax.ShapeDtypeStruct((m, n), BF16)]
    out_specs = [pl.BlockSpec((tm, tn), lambda i, j: (i, j))]
    if has_dt:
        args.append(w_dt)
        specs.append(pl.BlockSpec((d, LANES), lambda i, j: (0, 0)))
        out_shape.append(jax.ShapeDtypeStruct((m, LANES), F32))
        out_specs.append(pl.BlockSpec((tm, LANES), lambda i, j: (i, 0)))
    res = pl.pallas_call(
        functools.partial(_inproj_kernel, has_prev=has_prev, has_dt=has_dt),
        grid=(m // tm, n // tn),
        in_specs=specs,
        out_specs=out_specs,
        out_shape=out_shape,
        scratch_shapes=[pltpu.VMEM((tm, d), BF16)],
        compiler_params=_params(("parallel", "arbitrary")),
        name="in_projection",
    )(*args)
    return res if has_dt else (res[0], None)


def _qk_prep_kernel(y_ref, gain_ref, cos_ref, sin_ref, o_ref, *, use_norm):
    cos = cos_ref[...]
    sin = sin_ref[...]
    lane = lax.broadcasted_iota(jnp.int32, cos.shape, 1)
    first_half = (lane & (HEAD_DIM // 4)) == 0
    for hh in range(COL_TILE // HEAD_DIM):
        sl = slice(hh * HEAD_DIM, (hh + 1) * HEAD_DIM)
        v = y_ref[:, sl].astype(F32)
        if use_norm:
            v = v * lax.rsqrt(jnp.mean(v * v, axis=-1, keepdims=True) + NORM_EPS)
        v = v * gain_ref[:, sl]
        swapped = jnp.where(first_half, pltpu.roll(v, HEAD_DIM - HEAD_DIM // 4, 1), pltpu.roll(v, HEAD_DIM // 4, 1))
        o_ref[:, sl] = (v * cos + swapped * sin).astype(o_ref.dtype)


def qk_prep(yproj, gain_row, cos_t, sin_t, n_qk, pos_of_tile, use_norm):
    m = yproj.shape[0]
    tm = PREP_TILE
    return pl.pallas_call(
        functools.partial(_qk_prep_kernel, use_norm=use_norm),
        grid=(m // tm, n_qk // COL_TILE),
        in_specs=[
            pl.BlockSpec((tm, COL_TILE), lambda i, j: (i, j)),
            pl.BlockSpec((1, COL_TILE), lambda i, j: (0, j)),
            pl.BlockSpec((tm, HEAD_DIM), lambda i, j: (pos_of_tile(i), 0)),
            pl.BlockSpec((tm, HEAD_DIM), lambda i, j: (pos_of_tile(i), 0)),
        ],
        out_specs=pl.BlockSpec((tm, COL_TILE), lambda i, j: (i, j)),
        out_shape=jax.ShapeDtypeStruct((m, n_qk), BF16),
        compiler_params=_params(("parallel", "parallel")),
        name="qk_prep",
    )(yproj, gain_row, cos_t, sin_t)


def _attn_kernel(*refs, group, n_lat_tiles, seq, windowed, has_sink):
    it = iter(refs)
    if has_sink:
        sink_ref = next(it)
    q_ref, kc_ref, vc_ref, kl_ref, vl_ref, o_ref, m_scr, l_scr, acc_scr = (next(it) for _ in range(9))
    tq = ATTN_TQ
    kvh = pl.program_id(1)
    qi = pl.program_id(2)
    rows = group * tq

    q = jnp.concatenate([q_ref[:, g * HEAD_DIM:(g + 1) * HEAD_DIM] for g in range(group)], axis=0)

    if has_sink:
        for g in range(group):
            m_scr[g * tq:(g + 1) * tq, :] = jnp.full((tq, LANES), sink_ref[kvh * group + g], F32)
        l_scr[...] = jnp.ones((rows, LANES), F32)
    else:
        m_scr[...] = jnp.full((rows, LANES), -jnp.inf, F32)
        l_scr[...] = jnp.zeros((rows, LANES), F32)
    acc_scr[...] = jnp.zeros((rows, HEAD_DIM), F32)

    def step(k, v, mask):
        s = lax.dot_general(q, k, (((1,), (1,)), ((), ())), preferred_element_type=F32)
        if mask is not None:
            tk = s.shape[-1]
            s = jnp.where(mask[None], s.reshape(group, tq, tk), -jnp.inf).reshape(rows, tk)
        m_prev = m_scr[:, 0:1]
        m_new = jnp.maximum(m_prev, jnp.max(s, axis=-1, keepdims=True))
        alpha = jnp.exp(m_prev - m_new)
        p = jnp.exp(s - m_new)
        l_new = alpha * l_scr[:, 0:1] + jnp.sum(p, axis=-1, keepdims=True)
        acc_scr[...] = alpha * acc_scr[...] + jnp.dot(p.astype(v.dtype), v, preferred_element_type=F32)
        m_scr[...] = jnp.broadcast_to(m_new, (rows, LANES))
        l_scr[...] = jnp.broadcast_to(l_new, (rows, LANES))

    step(kc_ref[...], vc_ref[...], None)

    @pl.when(qi < n_lat_tiles)
    def _():
        if windowed:
            band = tq + 2 * C_WINDOW
            start = jnp.clip(qi * tq - C_WINDOW, 0, seq - band)
            start = pl.multiple_of(start, C_WINDOW)
            k_pos = start + lax.broadcasted_iota(jnp.int32, (tq, band), 1)
            q_pos = qi * tq + lax.broadcasted_iota(jnp.int32, (tq, band), 0)
            mask = jnp.abs(k_pos - q_pos) <= C_WINDOW
            step(kl_ref[pl.ds(start, band), :], vl_ref[pl.ds(start, band), :], mask)
        else:
            def body(c, carry):
                start = pl.multiple_of(c * ATTN_TK, ATTN_TK)
                step(kl_ref[pl.ds(start, ATTN_TK), :], vl_ref[pl.ds(start, ATTN_TK), :], None)
                return carry
            lax.fori_loop(0, seq // ATTN_TK, body, 0)

    out = acc_scr[...] / l_scr[:, 0:1]
    for g in range(group):
        o_ref[:, g * HEAD_DIM:(g + 1) * HEAD_DIM] = out[g * tq:(g + 1) * tq, :].astype(o_ref.dtype)


def attention(qk, yproj, sinks, *, batch, seq, ctx_len, n_heads, n_kv, v_col, windowed):
    m = qk.shape[0]
    group = n_heads // n_kv
    tq = ATTN_TQ
    assert ctx_len == tq and seq % ATTN_TK == 0 and seq >= tq + 2 * C_WINDOW
    n_lat_tiles = seq // tq
    ctx_block0 = batch * seq // ctx_len
    v_blk = v_col // HEAD_DIM
    has_sink = sinks is not None

    def q_map(b, h, i, *_):
        return (jnp.where(i < n_lat_tiles, b * n_lat_tiles + i, batch * n_lat_tiles + b), h)

    in_specs = [
        pl.BlockSpec((tq, group * HEAD_DIM), q_map),
        pl.BlockSpec((ctx_len, HEAD_DIM), lambda b, h, i, *_: (ctx_block0 + b, n_heads + h)),
        pl.BlockSpec((ctx_len, HEAD_DIM), lambda b, h, i, *_: (ctx_block0 + b, v_blk + h)),
        pl.BlockSpec((seq, HEAD_DIM), lambda b, h, i, *_: (b, n_heads + h)),
        pl.BlockSpec((seq, HEAD_DIM), lambda b, h, i, *_: (b, v_blk + h)),
    ]
    args = [qk, qk, yproj, qk, yproj]
    rows = group * tq
    grid_spec = pltpu.PrefetchScalarGridSpec(
        num_scalar_prefetch=1 if has_sink else 0,
        grid=(batch, n_kv, n_lat_tiles + 1),
        in_specs=in_specs,
        out_specs=pl.BlockSpec((tq, group * HEAD_DIM), q_map),
        scratch_shapes=[pltpu.VMEM((rows, LANES), F32), pltpu.VMEM((rows, LANES), F32),
                        pltpu.VMEM((rows, HEAD_DIM), F32)],
    )
    if has_sink:
        args = [sinks] + args
    return pl.pallas_call(
        functools.partial(_attn_kernel, group=group, n_lat_tiles=n_lat_tiles, seq=seq,
                          windowed=windowed, has_sink=has_sink),
        grid_spec=grid_spec,
        out_shape=jax.ShapeDtypeStruct((m, n_heads * HEAD_DIM), BF16),
        compiler_params=_params(("parallel", "parallel", "arbitrary")),
        name="attention_window" if windowed else "attention_global",
    )(*args)


CONV_HALO = 16


def _conv_kernel(prev_ref, cur_ref, next_ref, w_ref, b_ref, o_ref, *, tiles_per_seq, n_lat_tiles):
    i = pl.program_id(0)
    tm = cur_ref.shape[0]
    in_lat = i < n_lat_tiles
    pos = i % tiles_per_seq
    has_prev = jnp.logical_and(in_lat, pos != 0)
    has_next = jnp.logical_and(in_lat, pos != tiles_per_seq - 1)
    prev = jnp.where(has_prev, prev_ref[...].astype(F32), 0.0)
    nxt = jnp.where(has_next, next_ref[...].astype(F32), 0.0)
    ext = jnp.concatenate([prev, cur_ref[...].astype(F32), nxt], axis=0)
    n_ext = tm + 2 * CONV_HALO
    acc = jnp.zeros((tm, ext.shape[1]), F32) + b_ref[...]
    for k in range(B_CONV_W):
        shift = (B_CONV_W // 2 - k) % n_ext
        shifted = ext if shift == 0 else pltpu.roll(ext, shift, 0)
        acc = acc + shifted[CONV_HALO:CONV_HALO + tm, :] * w_ref[k:k + 1, :]
    o_ref[...] = (acc * _sigmoid(acc)).astype(o_ref.dtype)


def conv_silu(yproj, conv_w, conv_b, *, col0, seq, n_lat_rows):
    m = yproj.shape[0]
    tm = PREP_TILE
    tiles_per_seq = seq // tm
    n_lat_tiles = n_lat_rows // tm
    c0 = col0 // COL_TILE
    halo_per_tile = tm // CONV_HALO
    n_halo_blocks = m // CONV_HALO
    return pl.pallas_call(
        functools.partial(_conv_kernel, tiles_per_seq=tiles_per_seq, n_lat_tiles=n_lat_tiles),
        grid=(m // tm, B_CONV_CH // COL_TILE),
        in_specs=[
            pl.BlockSpec((CONV_HALO, COL_TILE), lambda i, j: (jnp.maximum(i * halo_per_tile - 1, 0), c0 + j)),
            pl.BlockSpec((tm, COL_TILE), lambda i, j: (i, c0 + j)),
            pl.BlockSpec((CONV_HALO, COL_TILE),
                         lambda i, j: (jnp.minimum((i + 1) * halo_per_tile, n_halo_blocks - 1), c0 + j)),
            pl.BlockSpec((8, COL_TILE), lambda i, j: (0, j)),
            pl.BlockSpec((1, COL_TILE), lambda i, j: (0, j)),
        ],
        out_specs=pl.BlockSpec((tm, COL_TILE), lambda i, j: (i, j)),
        out_shape=jax.ShapeDtypeStruct((m, B_CONV_CH), BF16),
        compiler_params=_params(("parallel", "parallel")),
        name="conv_silu",
    )(yproj, yproj, yproj, conv_w, conv_b)


def _ssd_kernel(xf_ref, bf_ref, cf_ref, dtf_ref, xb_ref, bb_ref, cb_ref, dtb_ref, bias_ref, alog_ref,
                yf_ref, yb_ref, h_scr):
    t = pl.program_id(1)
    q = B_CHUNK

    @pl.when(t == 0)
    def _():
        h_scr[...] = jnp.zeros(h_scr.shape, F32)

    row = lax.broadcasted_iota(jnp.int32, (q, q), 0)
    col = lax.broadcasted_iota(jnp.int32, (q, q), 1)
    lane_lo = lax.broadcasted_iota(jnp.int32, (q, LANES), 1) < B_HEAD_DIM
    lane_lo_row = lane_lo[0:1, :]
    a_coef = -jnp.exp(alog_ref[...])

    dirs = ((xf_ref, bf_ref, cf_ref, dtf_ref, yf_ref, col <= row, q - 1),
            (xb_ref, bb_ref, cb_ref, dtb_ref, yb_ref, col >= row, 0))
    for d, (x_ref, b_ref, c_ref, dt_ref, y_ref, tri, last_row) in enumerate(dirs):
        dt = _softplus(dt_ref[...] + bias_ref[...])
        ld = dt * a_coef
        cum = jnp.dot(tri.astype(F32), ld, preferred_element_type=F32, precision=lax.Precision.HIGHEST)
        cum_t = cum.T
        last = cum[last_row:last_row + 1, :]
        e_cum = jnp.exp(cum)
        e_rest = jnp.exp(last - cum)
        e_last = jnp.exp(last)

        def pick(arr, l0, lo=lane_lo):
            return jnp.where(lo, arr[:, l0:l0 + 1], arr[:, l0 + 1:l0 + 2])

        for g in range(B_GROUPS):
            bm = b_ref[:, g * B_STATE:(g + 1) * B_STATE]
            cm = c_ref[:, g * B_STATE:(g + 1) * B_STATE]
            cbm = lax.dot_general(cm, bm, (((1,), (1,)), ((), ())), preferred_element_type=F32)
            bm_t = bm.astype(F32).T.astype(BF16)
            h_prev = h_scr[d, g]
            y_off = jnp.dot(cm, h_prev.astype(BF16), preferred_element_type=F32)
            xw_parts = []
            dec_parts = []
            for p in range(B_HPG // 2):
                l0 = d * (B_GROUPS * B_HPG) + g * B_HPG + 2 * p
                c0 = g * B_GROUP_CH + p * LANES
                xdt = x_ref[:, c0:c0 + LANES].astype(F32) * pick(dt, l0)
                y_p = y_off[:, p * LANES:(p + 1) * LANES] * pick(e_cum, l0)
                for half in range(2):
                    l = l0 + half
                    seg = jnp.exp(jnp.where(tri, cum[:, l:l + 1] - cum_t[l:l + 1, :], -jnp.inf))
                    keep = lane_lo if half == 0 else jnp.logical_not(lane_lo)
                    y_p = y_p + jnp.dot((cbm * seg).astype(BF16), jnp.where(keep, xdt, 0.0).astype(BF16),
                                        preferred_element_type=F32)
                y_ref[:, c0:c0 + LANES] = y_p.astype(y_ref.dtype)
                xw_parts.append((xdt * pick(e_rest, l0)).astype(BF16))
                dec_parts.append(pick(e_last, l0, lane_lo_row))
            xw = jnp.concatenate(xw_parts, axis=1)
            dec = jnp.concatenate(dec_parts, axis=1)
            h_scr[d, g] = h_prev * dec + jnp.dot(bm_t, xw, preferred_element_type=F32)


def ssd_scan(xbc, dt_raw, dt_bias_row, a_log_row, *, batch, seq, ctx_len):
    m = xbc.shape[0]
    q = B_CHUNK
    nc_lat, nc_ctx = seq // q, ctx_len // q
    n_steps = nc_ctx + nc_lat
    ctx0 = batch * nc_lat

    def fwd_blk(b, t):
        return jnp.where(t < nc_ctx, ctx0 + b * nc_ctx + t, b * nc_lat + t - nc_ctx)

    def bwd_blk(b, t):
        return jnp.where(t < nc_ctx, ctx0 + b * nc_ctx + (nc_ctx - 1 - t), b * nc_lat + (n_steps - 1 - t))

    xb, bb, cb = 0, B_D_INNER // B_GN, B_D_INNER // B_GN + 1

    def specs(blk):
        return [
            pl.BlockSpec((q, B_D_INNER), lambda b, t: (blk(b, t), xb)),
            pl.BlockSpec((q, B_GN), lambda b, t: (blk(b, t), bb)),
            pl.BlockSpec((q, B_GN), lambda b, t: (blk(b, t), cb)),
            pl.BlockSpec((q, LANES), lambda b, t: (blk(b, t), 0)),
        ]

    row_spec = pl.BlockSpec((1, LANES), lambda b, t: (0, 0))
    return pl.pallas_call(
        _ssd_kernel,
        grid=(batch, n_steps),
        in_specs=specs(fwd_blk) + specs(bwd_blk) + [row_spec, row_spec],
        out_specs=[pl.BlockSpec((q, B_D_INNER), lambda b, t: (fwd_blk(b, t), 0)),
                   pl.BlockSpec((q, B_D_INNER), lambda b, t: (bwd_blk(b, t), 0))],
        out_shape=[jax.ShapeDtypeStruct((m, B_D_INNER), BF16)] * 2,
        scratch_shapes=[pltpu.VMEM((2, B_GROUPS, B_STATE, B_GROUP_CH), F32)],
        compiler_params=_params(("parallel", "arbitrary")),
        name="ssd_scan",
    )(xbc, xbc, xbc, dt_raw, xbc, xbc, xbc, dt_raw, dt_bias_row, a_log_row)


def _gated_norm_kernel(yf_ref, yb_ref, x_ref, z_ref, skip_ref, g_ref, o_ref):
    z = z_ref[...].astype(F32)
    y = yf_ref[...].astype(F32) + yb_ref[...].astype(F32) + skip_ref[...] * x_ref[...].astype(F32)
    yz = y * (z * _sigmoid(z))
    ms = jnp.mean(yz * yz, axis=-1, keepdims=True)
    o_ref[...] = (yz * lax.rsqrt(ms + NORM_EPS) * g_ref[...]).astype(o_ref.dtype)


def gated_norm(y_f, y_b, xbc, yproj, skip_row, g_row, *, z_col):
    m = y_f.shape[0]
    tm = PREP_TILE
    w = B_GROUP_CH
    z0 = z_col // w
    blk = pl.BlockSpec((tm, w), lambda i, j: (i, j))
    row = pl.BlockSpec((1, w), lambda i, j: (0, j))
    return pl.pallas_call(
        _gated_norm_kernel,
        grid=(m // tm, B_GROUPS),
        in_specs=[blk, blk, blk, pl.BlockSpec((tm, w), lambda i, j: (i, z0 + j)), row, row],
        out_specs=blk,
        out_shape=jax.ShapeDtypeStruct((m, B_D_INNER), BF16),
        compiler_params=_params(("parallel", "parallel")),
        name="gated_norm",
    )(y_f, y_b, xbc, yproj, skip_row, g_row)


def _outproj_kernel(*refs, n_lhs, has_prev):
    it = iter(refs)
    lhs = [next(it) for _ in range(n_lhs)]
    ws = [next(it) for _ in range(n_lhs)]
    x_ref = next(it)
    if has_prev:
        y_ref = next(it)
        pmod_ref = next(it)
    mod_ref = next(it)
    o_ref = next(it)
    acc = jnp.dot(lhs[0][...], ws[0][...], preferred_element_type=F32)
    for a, w in zip(lhs[1:], ws[1:]):
        acc = acc + jnp.dot(a[...], w[...], preferred_element_type=F32)
    x = x_ref[...]
    if has_prev:
        x = x + pmod_ref[0, 5:6, :] * y_ref[...].astype(F32)
    o_ref[...] = x + mod_ref[0, 2:3, :] * acc


def out_projection(lhs, w, x, prev, mods, mod_of_tile):
    m, d = x.shape
    tm, tn = ROW_TILE, COL_TILE
    kw = lhs[0].shape[1]
    has_prev = prev is not None
    mod_spec = pl.BlockSpec((1, N_MOD, tn), lambda i, j: (mod_of_tile(i), 0, j))
    specs = [pl.BlockSpec((tm, kw), lambda i, j: (i, 0)) for _ in lhs]
    specs += [pl.BlockSpec((kw, tn), functools.partial(lambda i, j, r: (r, j), r=r)) for r in range(len(lhs))]
    args = list(lhs) + [w] * len(lhs) + [x]
    specs.append(pl.BlockSpec((tm, tn), lambda i, j: (i, j)))
    if has_prev:
        args += [prev[0], prev[1]]
        specs += [pl.BlockSpec((tm, tn), lambda i, j: (i, j)), mod_spec]
    args.append(mods)
    specs.append(mod_spec)
    return pl.pallas_call(
        functools.partial(_outproj_kernel, n_lhs=len(lhs), has_prev=has_prev),
        grid=(m // tm, d // tn),
        in_specs=specs,
        out_specs=pl.BlockSpec((tm, tn), lambda i, j: (i, j)),
        out_shape=jax.ShapeDtypeStruct((m, d), F32),
        compiler_params=_params(("parallel", "parallel")),
        name="out_projection",
    )(*args)


def _ffn_prep_kernel(x_ref, mod_ref, g_ref, rw_ref, rb_ref, h_ref, gates_ref, gsteps_ref):
    h = _norm_mod(x_ref[...], g_ref[...], mod_ref[0, 3:4, :], mod_ref[0, 4:5, :])
    h_ref[...] = h.astype(h_ref.dtype)
    logits = jnp.dot(h, rw_ref[...], preferred_element_type=F32, precision=lax.Precision.HIGHEST) + rb_ref[...]
    lane = lax.broadcasted_iota(jnp.int32, logits.shape, 1)
    logits = jnp.where(lane < N_EXPERTS, logits, -jnp.inf)
    work = logits
    top = None
    denom = jnp.zeros((logits.shape[0], 1), F32)
    gates = jnp.zeros(logits.shape, F32)
    for _ in range(TOP_K):
        mx = jnp.max(work, axis=-1, keepdims=True)
        idx = jnp.min(jnp.where(work == mx, lane, LANES), axis=-1, keepdims=True)
        sel = lane == idx
        if top is None:
            top = mx
        e = jnp.exp(mx - top)
        denom = denom + e
        gates = gates + jnp.where(sel, e, 0.0)
        work = jnp.where(sel, -jnp.inf, work)
    gates = gates / denom
    gates_ref[...] = gates
    for s in range(N_EXPERT_STEPS):
        shift = (LANES - EXPERTS_PER_STEP * s) % LANES
        gsteps_ref[:, s * LANES:(s + 1) * LANES] = gates if shift == 0 else pltpu.roll(gates, shift, 1)


def ffn_prep(x, mods, g, router_w, router_b, mod_of_tile):
    m, d = x.shape
    tm = PREP_TILE
    tiles_per_row_tile = ROW_TILE // tm
    return pl.pallas_call(
        _ffn_prep_kernel,
        grid=(m // tm,),
        in_specs=[
            pl.BlockSpec((tm, d), lambda i: (i, 0)),
            pl.BlockSpec((1, N_MOD, d), lambda i: (mod_of_tile(i // tiles_per_row_tile), 0, 0)),
            pl.BlockSpec((1, d), lambda i: (0, 0)),
            pl.BlockSpec((d, LANES), lambda i: (0, 0)),
            pl.BlockSpec((1, LANES), lambda i: (0, 0)),
        ],
        out_specs=[pl.BlockSpec((tm, d), lambda i: (i, 0)),
                   pl.BlockSpec((tm, LANES), lambda i: (i, 0)),
                   pl.BlockSpec((tm, N_EXPERT_STEPS * LANES), lambda i: (i, 0))],
        out_shape=[jax.ShapeDtypeStruct((m, d), BF16),
                   jax.ShapeDtypeStruct((m, LANES), F32),
                   jax.ShapeDtypeStruct((m, N_EXPERT_STEPS * LANES), F32)],
        compiler_params=_params(("parallel",)),
        name="ffn_prep",
    )(x, mods, g.reshape(1, d), router_w, router_b)


def _moe_kernel(h_ref, gates_ref, gstep_ref, wgu_ref, bgu_ref, wdn_ref, bdn_ref, o_ref, acc_scr):
    s = pl.program_id(1)
    half = EXPERTS_PER_STEP * F_EXPERT

    @pl.when(s == 0)
    def _():
        acc_scr[...] = jnp.dot(gates_ref[...].astype(BF16), bdn_ref[...], preferred_element_type=F32)

    gu = jnp.dot(h_ref[...], wgu_ref[0], preferred_element_type=F32) + bgu_ref[0]
    glu = jnp.minimum(gu[:, :half], SWIGLU_LIMIT)
    lin = jnp.clip(gu[:, half:], -SWIGLU_LIMIT, SWIGLU_LIMIT)
    act = glu * _sigmoid(SWIGLU_ALPHA * glu) * (lin + 1.0)
    gs = gstep_ref[...]
    col = lax.broadcasted_iota(jnp.int32, act.shape, 1)
    act = act * jnp.where(col < F_EXPERT, gs[:, 0:1], gs[:, 1:2])
    acc_scr[...] += jnp.dot(act.astype(BF16), wdn_ref[0], preferred_element_type=F32)

    @pl.when(s == pl.num_programs(1) - 1)
    def _():
        o_ref[...] = acc_scr[...].astype(o_ref.dtype)


def moe_ffn(h, gates, gsteps, w_gu, b_gu, w_dn, b_dn):
    m, d = h.shape
    tm = ROW_TILE
    n_gu = 2 * EXPERTS_PER_STEP * F_EXPERT
    n_dn = EXPERTS_PER_STEP * F_EXPERT
    return pl.pallas_call(
        _moe_kernel,
        grid=(m // tm, N_EXPERT_STEPS),
        in_specs=[
            pl.BlockSpec((tm, d), lambda i, s: (i, 0)),
            pl.BlockSpec((tm, LANES), lambda i, s: (i, 0)),
            pl.BlockSpec((tm, LANES), lambda i, s: (i, s)),
            pl.BlockSpec((1, d, n_gu), lambda i, s: (s, 0, 0)),
            pl.BlockSpec((1, 1, n_gu), lambda i, s: (s, 0, 0)),
            pl.BlockSpec((1, n_dn, d), lambda i, s: (s, 0, 0)),
            pl.BlockSpec((LANES, d), lambda i, s: (0, 0)),
        ],
        out_specs=pl.BlockSpec((tm, d), lambda i, s: (i, 0)),
        out_shape=jax.ShapeDtypeStruct((m, d), BF16),
        scratch_shapes=[pltpu.VMEM((tm, d), F32)],
        compiler_params=_params(("parallel", "arbitrary")),
        name="moe_ffn",
    )(h, gates, gsteps, w_gu, b_gu, w_dn, b_dn)


def _final_kernel(x_ref, y_ref, mod_ref, g_ref, o_ref):
    x = x_ref[...] + mod_ref[0, 5:6, :] * y_ref[...].astype(F32)
    ms = jnp.mean(x * x, axis=-1, keepdims=True)
    o_ref[...] = x * lax.rsqrt(ms + NORM_EPS) * g_ref[...]


def final_norm(x, y, mods, g, n_rows, mod_of_tile):
    d = x.shape[1]
    tm = PREP_TILE
    tiles_per_row_tile = ROW_TILE // tm
    return pl.pallas_call(
        _final_kernel,
        grid=(n_rows // tm,),
        in_specs=[
            pl.BlockSpec((tm, d), lambda i: (i, 0)),
            pl.BlockSpec((tm, d), lambda i: (i, 0)),
            pl.BlockSpec((1, N_MOD, d), lambda i: (mod_of_tile(i // tiles_per_row_tile), 0, 0)),
            pl.BlockSpec((1, d), lambda i: (0, 0)),
        ],
        out_specs=pl.BlockSpec((tm, d), lambda i: (i, 0)),
        out_shape=jax.ShapeDtypeStruct((n_rows, d), F32),
        compiler_params=_params(("parallel",)),
        name="final_norm",
    )(x, y, mods, g.reshape(1, d))


def _rope_tables(seq, ctx_len):
    n_freq = HEAD_DIM // 4
    rows = seq // GRID_W
    row = jnp.repeat(jnp.arange(rows, dtype=F32), GRID_W)
    col = jnp.tile(jnp.arange(GRID_W, dtype=F32), rows)
    inv_freq = ROPE_THETA ** (-jnp.arange(n_freq, dtype=F32) / n_freq)
    ang_r = row[:, None] * inv_freq
    ang_c = col[:, None] * inv_freq
    cos = jnp.concatenate([jnp.cos(ang_r)] * 2 + [jnp.cos(ang_c)] * 2, axis=-1)
    sin = jnp.concatenate([-jnp.sin(ang_r), jnp.sin(ang_r), -jnp.sin(ang_c), jnp.sin(ang_c)], axis=-1)
    cos = jnp.concatenate([jnp.ones((ctx_len, HEAD_DIM), F32), cos], axis=0)
    sin = jnp.concatenate([jnp.zeros((ctx_len, HEAD_DIM), F32), sin], axis=0)
    return cos, sin


def _pad_lanes(v):
    v = v.reshape(1, -1).astype(F32)
    return jnp.pad(v, ((0, 0), (0, LANES - v.shape[1])))


def _expert_weights(w_gu, b_gu, w_dn):
    e, d, _ = w_gu.shape
    s, p = N_EXPERT_STEPS, EXPERTS_PER_STEP
    glu = w_gu[..., 0::2].reshape(s, p, d, F_EXPERT)
    lin = w_gu[..., 1::2].reshape(s, p, d, F_EXPERT)
    wgu = jnp.concatenate([glu, lin], axis=1)
    wgu = jnp.moveaxis(wgu, 1, 2).reshape(s, d, 2 * p * F_EXPERT).astype(BF16)
    bglu = b_gu[..., 0::2].reshape(s, p * F_EXPERT)
    blin = b_gu[..., 1::2].reshape(s, p * F_EXPERT)
    bgu = jnp.concatenate([bglu, blin], axis=-1).reshape(s, 1, 2 * p * F_EXPERT)
    wdn = w_dn.reshape(s, p * F_EXPERT, d).astype(BF16)
    return wgu, bgu, wdn


def kernel(x, c, ctx, c_ctx, ada_w, ada_b, norm_mix_g, norm_ffn_g, router_w, router_b, exp_w_gu, exp_b_gu,
           exp_w_down, exp_b_down, ev_w_in, ev_w_out, ev_q_g, ev_k_g, ev_conv_w, ev_conv_b, ev_a_log,
           ev_dt_bias, ev_d_skip, ev_ssm_g, od_w_in, od_w_out, od_sinks, final_g):
    batch, seq, d = x.shape
    ctx_len = ctx.shape[1]
    n_lat = batch * seq
    m = n_lat + batch * ctx_len
    assert d == D_MODEL and seq % ROW_TILE == 0 and batch * ctx_len == ROW_TILE and ctx_len == ATTN_TQ
    assert batch + 1 <= 8

    xs = jnp.concatenate([x.reshape(n_lat, d), ctx.reshape(batch * ctx_len, d)], axis=0)
    cvecs = jnp.concatenate([c, c_ctx[None, :], jnp.zeros((8 - batch - 1, d), F32)], axis=0)
    mods_all = ada_mods(cvecs, ada_w, ada_b).reshape(ada_w.shape[0], 8, N_MOD, d)[:, :batch + 1]

    tiles_per_batch = seq // ROW_TILE

    def mod_of_tile(i):
        return jnp.minimum(i // tiles_per_batch, batch)

    prep_ctx0 = n_lat // PREP_TILE
    pos_tiles = seq // PREP_TILE

    def pos_of_tile(i):
        return jnp.where(i < prep_ctx0, 1 + i % pos_tiles, 0)

    cos_t, sin_t = _rope_tables(seq, ctx_len)
    scale = HEAD_DIM ** -0.5

    prev = None
    for i in range(DEPTH):
        p = i // 2
        mods = mods_all[i]
        if i % 2 == 0:
            w_in = ev_w_in[p]
            w_main = w_in[:, :EVEN_MAIN].astype(BF16)
            w_dt = jnp.pad(w_in[:, EVEN_MAIN:], ((0, 0), (0, LANES - 2 * B_HEADS))).astype(BF16)
            yproj, dt_raw = in_projection(xs, prev, mods, norm_mix_g[i], w_main, w_dt, mod_of_tile)
            gain = jnp.concatenate([jnp.tile(ev_q_g[p], A_HEADS) * scale, jnp.tile(ev_k_g[p], A_KV_HEADS)])
            qk = qk_prep(yproj, gain.reshape(1, -1), cos_t, sin_t, EVEN_QK, pos_of_tile, True)
            attn = attention(qk, yproj, None, batch=batch, seq=seq, ctx_len=ctx_len, n_heads=A_HEADS,
                             n_kv=A_KV_HEADS, v_col=EVEN_QK, windowed=False)
            z_col = A_Q + 2 * A_KV
            conv_w = jnp.pad(ev_conv_w[p], ((0, 8 - B_CONV_W), (0, 0)))
            xbc = conv_silu(yproj, conv_w, ev_conv_b[p].reshape(1, -1), col0=z_col + B_D_INNER, seq=seq,
                            n_lat_rows=n_lat)
            y_f, y_b = ssd_scan(xbc, dt_raw, _pad_lanes(ev_dt_bias[p]), _pad_lanes(ev_a_log[p]),
                                batch=batch, seq=seq, ctx_len=ctx_len)
            skip_row = jnp.repeat(ev_d_skip[p, 0] + ev_d_skip[p, 1], B_HEAD_DIM).reshape(1, -1)
            ssm = gated_norm(y_f, y_b, xbc, yproj, skip_row, ev_ssm_g[p].reshape(1, -1), z_col=z_col)
            xs = out_projection([attn, ssm], ev_w_out[p].astype(BF16), xs, prev, mods, mod_of_tile)
        else:
            yproj, _ = in_projection(xs, prev, mods, norm_mix_g[i], od_w_in[p].astype(BF16), None, mod_of_tile)
            gain = jnp.concatenate([jnp.full((C_Q,), scale, F32), jnp.ones((C_KV,), F32)])
            qk = qk_prep(yproj, gain.reshape(1, -1), cos_t, sin_t, ODD_QK, pos_of_tile, False)
            attn = attention(qk, yproj, od_sinks[p], batch=batch, seq=seq, ctx_len=ctx_len, n_heads=C_HEADS,
                             n_kv=C_KV_HEADS, v_col=ODD_QK, windowed=True)
            xs = out_projection([attn], od_w_out[p].astype(BF16), xs, prev, mods, mod_of_tile)
        rw = jnp.pad(router_w[i], ((0, 0), (0, LANES - N_EXPERTS)))
        h, gates, gsteps = ffn_prep(xs, mods, norm_ffn_g[i], rw, _pad_lanes(router_b[i]), mod_of_tile)
        wgu, bgu, wdn = _expert_weights(exp_w_gu[i], exp_b_gu[i], exp_w_down[i])
        bdn = jnp.pad(exp_b_down[i], ((0, LANES - N_EXPERTS), (0, 0))).astype(BF16)
        y = moe_ffn(h, gates, gsteps, wgu, bgu, wdn, bdn)
        prev = (y, mods)

    out = final_norm(xs, prev[0], prev[1], final_g, n_lat, mod_of_tile)
    return out.reshape(batch, seq, d)
```

```python
import functools
import math

import jax
import jax.numpy as jnp
import numpy as np
from jax import lax
from jax.experimental import pallas as pl
from jax.experimental.pallas import tpu as pltpu

F32 = jnp.float32
BF16 = jnp.bfloat16

D_MODEL = 4096
HEAD_DIM = 128
GRID_W = 64
ROPE_THETA = 10000.0
NORM_EPS = 1e-6
N_MOD = 6
DEPTH = 4

A_HEADS = D_MODEL // (2 * HEAD_DIM)
A_KV_HEADS = A_HEADS // 4
A_Q = A_HEADS * HEAD_DIM
A_KV = A_KV_HEADS * HEAD_DIM

B_D_INNER = D_MODEL // 2
B_HEAD_DIM = 64
B_HEADS = B_D_INNER // B_HEAD_DIM
B_GROUPS = 4
B_HPG = B_HEADS // B_GROUPS
B_STATE = 128
B_GN = B_GROUPS * B_STATE
B_CONV_CH = B_D_INNER + 2 * B_GN
B_CONV_W = 5
B_CHUNK = 128
B_GROUP_CH = B_D_INNER // B_GROUPS

C_HEADS = D_MODEL // HEAD_DIM
C_KV_HEADS = C_HEADS // 4
C_Q = C_HEADS * HEAD_DIM
C_KV = C_KV_HEADS * HEAD_DIM
C_WINDOW = 128

N_EXPERTS = 32
TOP_K = 4
F_EXPERT = 192
SWIGLU_LIMIT = 7.0
SWIGLU_ALPHA = 1.702
EXPERTS_PER_STEP = 2
N_EXPERT_STEPS = N_EXPERTS // EXPERTS_PER_STEP

LANES = 128
ROW_TILE = 512
COL_TILE = 512
ATTN_TQ = 256
ATTN_TK = 512
ATTN_L_ROWS = 16
LOG2_E = math.log2(math.e)
PREP_TILE = 256
VMEM_LIMIT = 56 * 1024 * 1024

EVEN_MAIN = A_Q + 2 * A_KV + B_D_INNER + B_CONV_CH
EVEN_QK = A_Q + A_KV
ODD_IN = C_Q + 2 * C_KV
ODD_QK = C_Q + C_KV


def _params(sem):
    return pltpu.CompilerParams(dimension_semantics=sem, vmem_limit_bytes=VMEM_LIMIT)


def _sigmoid(v):
    return 1.0 / (1.0 + jnp.exp(-v))


def _softplus(v):
    return jnp.maximum(v, 0.0) + jnp.log(1.0 + jnp.exp(-jnp.abs(v)))


def _ada_kernel(c_ref, w_ref, b_ref, o_ref):
    c = c_ref[...]
    s = (c * _sigmoid(c)).astype(BF16)
    o_ref[0] = jnp.dot(s, w_ref[0].astype(BF16), preferred_element_type=F32) + b_ref[0]


def ada_mods(cvecs, ada_w, ada_b):
    depth, d, n = ada_w.shape
    tn = COL_TILE
    return pl.pallas_call(
        _ada_kernel,
        grid=(depth, n // tn),
        in_specs=[
            pl.BlockSpec((8, d), lambda l, j: (0, 0)),
            pl.BlockSpec((1, d, tn), lambda l, j: (l, 0, j)),
            pl.BlockSpec((1, 1, tn), lambda l, j: (l, 0, j)),
        ],
        out_specs=pl.BlockSpec((1, 8, tn), lambda l, j: (l, 0, j)),
        out_shape=jax.ShapeDtypeStruct((depth, 8, n), F32),
        compiler_params=_params(("parallel", "parallel")),
        name="ada_mods",
    )(cvecs, ada_w, ada_b.reshape(depth, 1, n))


def _norm_mod(x, g, shift, scale):
    ms = jnp.mean(x * x, axis=-1, keepdims=True)
    return (x * lax.rsqrt(ms + NORM_EPS) * g) * (1.0 + scale) + shift


def _inproj_kernel(*refs, has_prev, has_dt):
    it = iter(refs)
    x_ref = next(it)
    if has_prev:
        y_ref = next(it)
        pmod_ref = next(it)
    mod_ref = next(it)
    g_ref = next(it)
    w_ref = next(it)
    if has_dt:
        wdt_ref = next(it)
    o_ref = next(it)
    if has_dt:
        dt_ref = next(it)
    h_scr = next(it)

    @pl.when(pl.program_id(1) == 0)
    def _():
        x = x_ref[...]
        if has_prev:
            x = x + pmod_ref[0, 5:6, :] * y_ref[...].astype(F32)
        h = _norm_mod(x, g_ref[...], mod_ref[0, 0:1, :], mod_ref[0, 1:2, :])
        h_scr[...] = h.astype(BF16)
        if has_dt:
            dt_ref[...] = jnp.dot(h_scr[...], wdt_ref[...], preferred_element_type=F32)

    o_ref[...] = jnp.dot(h_scr[...], w_ref[...], preferred_element_type=F32).astype(o_ref.dtype)


def in_projection(x, prev, mods, g, w, w_dt, mod_of_tile):
    m, d = x.shape
    n = w.shape[1]
    tm, tn = ROW_TILE, COL_TILE
    has_prev = prev is not None
    has_dt = w_dt is not None
    mod_spec = pl.BlockSpec((1, N_MOD, d), lambda i, j: (mod_of_tile(i), 0, 0))
    args = [x]
    specs = [pl.BlockSpec((tm, d), lambda i, j: (i, 0))]
    if has_prev:
        args += [prev[0], prev[1]]
        specs += [pl.BlockSpec((tm, d), lambda i, j: (i, 0)), mod_spec]
    args += [mods, g.reshape(1, d), w]
    specs += [mod_spec, pl.BlockSpec((1, d), lambda i, j: (0, 0)), pl.BlockSpec((d, tn), lambda i, j: (0, j))]
    out_shape = [jax.ShapeDtypeStruct((m, n), BF16)]
    out_specs = [pl.BlockSpec((tm, tn), lambda i, j: (i, j))]
    if has_dt:
        args.append(w_dt)
        specs.append(pl.BlockSpec((d, LANES), lambda i, j: (0, 0)))
        out_shape.append(jax.ShapeDtypeStruct((m, LANES), F32))
        out_specs.append(pl.BlockSpec((tm, LANES), lambda i, j: (i, 0)))
    res = pl.pallas_call(
        functools.partial(_inproj_kernel, has_prev=has_prev, has_dt=has_dt),
        grid=(m // tm, n // tn),
        in_specs=specs,
        out_specs=out_specs,
        out_shape=out_shape,
        scratch_shapes=[pltpu.VMEM((tm, d), BF16)],
        compiler_params=_params(("parallel", "arbitrary")),
        name="in_projection",
    )(*args)
    return res if has_dt else (res[0], None)


def _head_prep_kernel(*refs, use_norm, use_rope, transpose):
    y_ref = refs[0]
    o_ref = refs[-1]
    if use_rope:
        gain_ref, cos_ref, sin_ref = refs[1:4]
        cos = cos_ref[...]
        sin = sin_ref[...]
        lane = lax.broadcasted_iota(jnp.int32, cos.shape, 1)
        first_half = (lane & (HEAD_DIM // 4)) == 0
    for hh in range(COL_TILE // HEAD_DIM):
        sl = slice(hh * HEAD_DIM, (hh + 1) * HEAD_DIM)
        v = y_ref[:, sl].astype(F32)
        if use_norm:
            v = v * lax.rsqrt(jnp.mean(v * v, axis=-1, keepdims=True) + NORM_EPS)
        if use_rope:
            v = v * gain_ref[:, sl]
            swapped = jnp.where(first_half, pltpu.roll(v, HEAD_DIM - HEAD_DIM // 4, 1),
                                pltpu.roll(v, HEAD_DIM // 4, 1))
            v = v * cos + swapped * sin
        if transpose:
            o_ref[sl, :] = v.T.astype(o_ref.dtype)
        else:
            o_ref[:, sl] = v.astype(o_ref.dtype)


def head_prep(yproj, col0, n_cols, rope, pos_of_tile, *, use_norm, transpose):
    m = yproj.shape[0]
    tm = PREP_TILE
    c0 = col0 // COL_TILE
    use_rope = rope is not None
    args = [yproj]
    specs = [pl.BlockSpec((tm, COL_TILE), lambda i, j: (i, c0 + j))]
    if use_rope:
        args += list(rope)
        specs += [pl.BlockSpec((1, COL_TILE), lambda i, j: (0, j)),
                  pl.BlockSpec((tm, HEAD_DIM), lambda i, j: (pos_of_tile(i), 0)),
                  pl.BlockSpec((tm, HEAD_DIM), lambda i, j: (pos_of_tile(i), 0))]
    if transpose:
        out_spec = pl.BlockSpec((COL_TILE, tm), lambda i, j: (j, i))
        out_shape = jax.ShapeDtypeStruct((n_cols, m), BF16)
    else:
        out_spec = pl.BlockSpec((tm, COL_TILE), lambda i, j: (i, j))
        out_shape = jax.ShapeDtypeStruct((m, n_cols), BF16)
    return pl.pallas_call(
        functools.partial(_head_prep_kernel, use_norm=use_norm, use_rope=use_rope, transpose=transpose),
        grid=(m // tm, n_cols // COL_TILE),
        in_specs=specs,
        out_specs=out_spec,
        out_shape=out_shape,
        compiler_params=_params(("parallel", "parallel")),
        name="head_prep",
    )(*args)


def _attn_kernel(*refs, group, n_lat_tiles, seq, windowed, has_sink):
    it = iter(refs)
    if has_sink:
        sink_ref = next(it)
    qt_ref, kc_ref, vtc_ref, kl_ref, vtl_ref, o_ref, m_scr, acc_scr = (next(it) for _ in range(8))
    tq = ATTN_TQ
    kvh = pl.program_id(1)
    qi = pl.program_id(2)
    cols = group * tq

    q_t = jnp.concatenate([qt_ref[g * HEAD_DIM:(g + 1) * HEAD_DIM, :] for g in range(group)], axis=1)

    if has_sink:
        m_scr[...] = jnp.concatenate(
            [jnp.full((1, tq), sink_ref[kvh * group + g] * LOG2_E, F32) for g in range(group)], axis=1)
        acc_scr[...] = jnp.concatenate([jnp.zeros((HEAD_DIM, cols), F32), jnp.ones((ATTN_L_ROWS, cols), F32)], axis=0)
    else:
        m_scr[...] = jnp.full((1, cols), -jnp.inf, F32)
        acc_scr[...] = jnp.zeros((HEAD_DIM + ATTN_L_ROWS, cols), F32)

    def step(k, v_t, mask):
        s = jnp.dot(k, q_t, preferred_element_type=F32)
        if mask is not None:
            s = jnp.where(mask, s, -jnp.inf)
        m_prev = m_scr[...]
        m_new = jnp.maximum(m_prev, jnp.max(s, axis=0, keepdims=True))
        alpha = jnp.exp2(m_prev - m_new)
        p = jnp.exp2(s - m_new).astype(BF16)
        v_ext = jnp.concatenate([v_t, jnp.ones((ATTN_L_ROWS, v_t.shape[1]), BF16)], axis=0)
        acc_scr[...] = alpha * acc_scr[...] + jnp.dot(v_ext, p, preferred_element_type=F32)
        m_scr[...] = m_new

    step(kc_ref[...], vtc_ref[...], None)

    @pl.when(qi < n_lat_tiles)
    def _():
        if windowed:
            band = tq + 2 * C_WINDOW
            start = jnp.clip(qi * tq - C_WINDOW, 0, seq - band)
            start = pl.multiple_of(start, C_WINDOW)
            k_pos = start + lax.broadcasted_iota(jnp.int32, (band, cols), 0)
            q_pos = qi * tq + (lax.broadcasted_iota(jnp.int32, (band, cols), 1) & (tq - 1))
            mask = jnp.abs(k_pos - q_pos) <= C_WINDOW
            step(kl_ref[pl.ds(start, band), :], vtl_ref[:, pl.ds(start, band)], mask)
        else:
            def body(c, carry):
                start = pl.multiple_of(c * ATTN_TK, ATTN_TK)
                step(kl_ref[pl.ds(start, ATTN_TK), :], vtl_ref[:, pl.ds(start, ATTN_TK)], None)
                return carry
            lax.fori_loop(0, seq // ATTN_TK, body, 0, unroll=2)

    out_t = acc_scr[0:HEAD_DIM, :] / acc_scr[HEAD_DIM:HEAD_DIM + 1, :]
    for g in range(group):
        o_ref[:, g * HEAD_DIM:(g + 1) * HEAD_DIM] = out_t[:, g * tq:(g + 1) * tq].T.astype(o_ref.dtype)


def attention(q_t, k, v_t, sinks, *, batch, seq, ctx_len, n_heads, n_kv, windowed):
    m = k.shape[0]
    group = n_heads // n_kv
    tq = ATTN_TQ
    assert ctx_len == tq and seq % ATTN_TK == 0 and seq >= tq + 2 * C_WINDOW and tq & (tq - 1) == 0
    n_lat_tiles = seq // tq
    ctx_block0 = batch * seq // ctx_len
    has_sink = sinks is not None

    def q_tile(b, i):
        return jnp.where(i < n_lat_tiles, b * n_lat_tiles + i, batch * n_lat_tiles + b)

    in_specs = [
        pl.BlockSpec((group * HEAD_DIM, tq), lambda b, h, i, *_: (h, q_tile(b, i))),
        pl.BlockSpec((ctx_len, HEAD_DIM), lambda b, h, i, *_: (ctx_block0 + b, h)),
        pl.BlockSpec((HEAD_DIM, ctx_len), lambda b, h, i, *_: (h, ctx_block0 + b)),
        pl.BlockSpec((seq, HEAD_DIM), lambda b, h, i, *_: (b, h)),
        pl.BlockSpec((HEAD_DIM, seq), lambda b, h, i, *_: (h, b)),
    ]
    args = [q_t, k, v_t, k, v_t]
    cols = group * tq
    grid_spec = pltpu.PrefetchScalarGridSpec(
        num_scalar_prefetch=1 if has_sink else 0,
        grid=(batch, n_kv, n_lat_tiles + 1),
        in_specs=in_specs,
        out_specs=pl.BlockSpec((tq, group * HEAD_DIM), lambda b, h, i, *_: (q_tile(b, i), h)),
        scratch_shapes=[pltpu.VMEM((1, cols), F32), pltpu.VMEM((HEAD_DIM + ATTN_L_ROWS, cols), F32)],
    )
    if has_sink:
        args = [sinks] + args
    return pl.pallas_call(
        functools.partial(_attn_kernel, group=group, n_lat_tiles=n_lat_tiles, seq=seq,
                          windowed=windowed, has_sink=has_sink),
        grid_spec=grid_spec,
        out_shape=jax.ShapeDtypeStruct((m, n_heads * HEAD_DIM), BF16),
        compiler_params=_params(("parallel", "parallel", "arbitrary")),
        name="attention_window" if windowed else "attention_global",
    )(*args)


CONV_HALO = 16


def _conv_kernel(prev_ref, cur_ref, next_ref, w_ref, b_ref, o_ref, *, tiles_per_seq, n_lat_tiles):
    i = pl.program_id(0)
    tm = cur_ref.shape[0]
    in_lat = i < n_lat_tiles
    pos = i % tiles_per_seq
    has_prev = jnp.logical_and(in_lat, pos != 0)
    has_next = jnp.logical_and(in_lat, pos != tiles_per_seq - 1)
    prev = jnp.where(has_prev, prev_ref[...].astype(F32), 0.0)
    nxt = jnp.where(has_next, next_ref[...].astype(F32), 0.0)
    ext = jnp.concatenate([prev, cur_ref[...].astype(F32), nxt], axis=0)
    n_ext = tm + 2 * CONV_HALO
    acc = jnp.zeros((tm, ext.shape[1]), F32) + b_ref[...]
    for k in range(B_CONV_W):
        shift = (B_CONV_W // 2 - k) % n_ext
        shifted = ext if shift == 0 else pltpu.roll(ext, shift, 0)
        acc = acc + shifted[CONV_HALO:CONV_HALO + tm, :] * w_ref[k:k + 1, :]
    o_ref[...] = (acc * _sigmoid(acc)).astype(o_ref.dtype)


def conv_silu(yproj, conv_w, conv_b, *, col0, seq, n_lat_rows):
    m = yproj.shape[0]
    tm = PREP_TILE
    tiles_per_seq = seq // tm
    n_lat_tiles = n_lat_rows // tm
    c0 = col0 // COL_TILE
    halo_per_tile = tm // CONV_HALO
    n_halo_blocks = m // CONV_HALO
    return pl.pallas_call(
        functools.partial(_conv_kernel, tiles_per_seq=tiles_per_seq, n_lat_tiles=n_lat_tiles),
        grid=(m // tm, B_CONV_CH // COL_TILE),
        in_specs=[
            pl.BlockSpec((CONV_HALO, COL_TILE), lambda i, j: (jnp.maximum(i * halo_per_tile - 1, 0), c0 + j)),
            pl.BlockSpec((tm, COL_TILE), lambda i, j: (i, c0 + j)),
            pl.BlockSpec((CONV_HALO, COL_TILE),
                         lambda i, j: (jnp.minimum((i + 1) * halo_per_tile, n_halo_blocks - 1), c0 + j)),
            pl.BlockSpec((8, COL_TILE), lambda i, j: (0, j)),
            pl.BlockSpec((1, COL_TILE), lambda i, j: (0, j)),
        ],
        out_specs=pl.BlockSpec((tm, COL_TILE), lambda i, j: (i, j)),
        out_shape=jax.ShapeDtypeStruct((m, B_CONV_CH), BF16),
        compiler_params=_params(("parallel", "parallel")),
        name="conv_silu",
    )(yproj, yproj, yproj, conv_w, conv_b)


def _ssd_kernel(xf_ref, bf_ref, cf_ref, dtf_ref, xb_ref, bb_ref, cb_ref, dtb_ref, bias_ref, alog_ref,
                yf_ref, yb_ref, h_scr):
    t = pl.program_id(1)
    q = B_CHUNK

    @pl.when(t == 0)
    def _():
        h_scr[...] = jnp.zeros(h_scr.shape, F32)

    row = lax.broadcasted_iota(jnp.int32, (q, q), 0)
    col = lax.broadcasted_iota(jnp.int32, (q, q), 1)
    lane_lo = lax.broadcasted_iota(jnp.int32, (q, LANES), 1) < B_HEAD_DIM
    lane_lo_row = lane_lo[0:1, :]
    a_coef = -jnp.exp(alog_ref[...])

    dirs = ((xf_ref, bf_ref, cf_ref, dtf_ref, yf_ref, col <= row, q - 1),
            (xb_ref, bb_ref, cb_ref, dtb_ref, yb_ref, col >= row, 0))
    for d, (x_ref, b_ref, c_ref, dt_ref, y_ref, tri, last_row) in enumerate(dirs):
        dt = _softplus(dt_ref[...] + bias_ref[...])
        ld = dt * a_coef
        cum = jnp.dot(tri.astype(F32), ld, preferred_element_type=F32, precision=lax.Precision.HIGHEST)
        cum_t = cum.T
        last = cum[last_row:last_row + 1, :]
        e_cum = jnp.exp(cum)
        e_rest = jnp.exp(last - cum)
        e_last = jnp.exp(last)

        def pick(arr, l0, lo=lane_lo):
            return jnp.where(lo, arr[:, l0:l0 + 1], arr[:, l0 + 1:l0 + 2])

        for g in range(B_GROUPS):
            bm = b_ref[:, g * B_STATE:(g + 1) * B_STATE]
            cm = c_ref[:, g * B_STATE:(g + 1) * B_STATE]
            cbm = lax.dot_general(cm, bm, (((1,), (1,)), ((), ())), preferred_element_type=F32)
            bm_t = bm.astype(F32).T.astype(BF16)
            h_prev = h_scr[d, g]
            y_off = jnp.dot(cm, h_prev.astype(BF16), preferred_element_type=F32)
            xw_parts = []
            dec_parts = []
            for p in range(B_HPG // 2):
                l0 = d * (B_GROUPS * B_HPG) + g * B_HPG + 2 * p
                c0 = g * B_GROUP_CH + p * LANES
                xdt = x_ref[:, c0:c0 + LANES].astype(F32) * pick(dt, l0)
                y_p = y_off[:, p * LANES:(p + 1) * LANES] * pick(e_cum, l0)
                for half in range(2):
                    l = l0 + half
                    seg = jnp.exp(jnp.where(tri, cum[:, l:l + 1] - cum_t[l:l + 1, :], -jnp.inf))
                    keep = lane_lo if half == 0 else jnp.logical_not(lane_lo)
                    y_p = y_p + jnp.dot((cbm * seg).astype(BF16), jnp.where(keep, xdt, 0.0).astype(BF16),
                                        preferred_element_type=F32)
                y_ref[:, c0:c0 + LANES] = y_p.astype(y_ref.dtype)
                xw_parts.append((xdt * pick(e_rest, l0)).astype(BF16))
                dec_parts.append(pick(e_last, l0, lane_lo_row))
            xw = jnp.concatenate(xw_parts, axis=1)
            dec = jnp.concatenate(dec_parts, axis=1)
            h_scr[d, g] = h_prev * dec + jnp.dot(bm_t, xw, preferred_element_type=F32)


def ssd_scan(xbc, dt_raw, dt_bias_row, a_log_row, *, batch, seq, ctx_len):
    m = xbc.shape[0]
    q = B_CHUNK
    nc_lat, nc_ctx = seq // q, ctx_len // q
    n_steps = nc_ctx + nc_lat
    ctx0 = batch * nc_lat

    def fwd_blk(b, t):
        return jnp.where(t < nc_ctx, ctx0 + b * nc_ctx + t, b * nc_lat + t - nc_ctx)

    def bwd_blk(b, t):
        return jnp.where(t < nc_ctx, ctx0 + b * nc_ctx + (nc_ctx - 1 - t), b * nc_lat + (n_steps - 1 - t))

    xb, bb, cb = 0, B_D_INNER // B_GN, B_D_INNER // B_GN + 1

    def specs(blk):
        return [
            pl.BlockSpec((q, B_D_INNER), lambda b, t: (blk(b, t), xb)),
            pl.BlockSpec((q, B_GN), lambda b, t: (blk(b, t), bb)),
            pl.BlockSpec((q, B_GN), lambda b, t: (blk(b, t), cb)),
            pl.BlockSpec((q, LANES), lambda b, t: (blk(b, t), 0)),
        ]

    row_spec = pl.BlockSpec((1, LANES), lambda b, t: (0, 0))
    return pl.pallas_call(
        _ssd_kernel,
        grid=(batch, n_steps),
        in_specs=specs(fwd_blk) + specs(bwd_blk) + [row_spec, row_spec],
        out_specs=[pl.BlockSpec((q, B_D_INNER), lambda b, t: (fwd_blk(b, t), 0)),
                   pl.BlockSpec((q, B_D_INNER), lambda b, t: (bwd_blk(b, t), 0))],
        out_shape=[jax.ShapeDtypeStruct((m, B_D_INNER), BF16)] * 2,
        scratch_shapes=[pltpu.VMEM((2, B_GROUPS, B_STATE, B_GROUP_CH), F32)],
        compiler_params=_params(("parallel", "arbitrary")),
        name="ssd_scan",
    )(xbc, xbc, xbc, dt_raw, xbc, xbc, xbc, dt_raw, dt_bias_row, a_log_row)


def _gated_norm_kernel(yf_ref, yb_ref, x_ref, z_ref, skip_ref, g_ref, o_ref):
    z = z_ref[...].astype(F32)
    y = yf_ref[...].astype(F32) + yb_ref[...].astype(F32) + skip_ref[...] * x_ref[...].astype(F32)
    yz = y * (z * _sigmoid(z))
    ms = jnp.mean(yz * yz, axis=-1, keepdims=True)
    o_ref[...] = (yz * lax.rsqrt(ms + NORM_EPS) * g_ref[...]).astype(o_ref.dtype)


def gated_norm(y_f, y_b, xbc, yproj, skip_row, g_row, *, z_col):
    m = y_f.shape[0]
    tm = PREP_TILE
    w = B_GROUP_CH
    z0 = z_col // w
    blk = pl.BlockSpec((tm, w), lambda i, j: (i, j))
    row = pl.BlockSpec((1, w), lambda i, j: (0, j))
    return pl.pallas_call(
        _gated_norm_kernel,
        grid=(m // tm, B_GROUPS),
        in_specs=[blk, blk, blk, pl.BlockSpec((tm, w), lambda i, j: (i, z0 + j)), row, row],
        out_specs=blk,
        out_shape=jax.ShapeDtypeStruct((m, B_D_INNER), BF16),
        compiler_params=_params(("parallel", "parallel")),
        name="gated_norm",
    )(y_f, y_b, xbc, yproj, skip_row, g_row)


def _outproj_kernel(*refs, n_lhs, has_prev):
    it = iter(refs)
    lhs = [next(it) for _ in range(n_lhs)]
    ws = [next(it) for _ in range(n_lhs)]
    x_ref = next(it)
    if has_prev:
        y_ref = next(it)
        pmod_ref = next(it)
    mod_ref = next(it)
    o_ref = next(it)
    acc = jnp.dot(lhs[0][...], ws[0][...], preferred_element_type=F32)
    for a, w in zip(lhs[1:], ws[1:]):
        acc = acc + jnp.dot(a[...], w[...], preferred_element_type=F32)
    x = x_ref[...]
    if has_prev:
        x = x + pmod_ref[0, 5:6, :] * y_ref[...].astype(F32)
    o_ref[...] = x + mod_ref[0, 2:3, :] * acc


def out_projection(lhs, w, x, prev, mods, mod_of_tile):
    m, d = x.shape
    tm, tn = ROW_TILE, COL_TILE
    kw = lhs[0].shape[1]
    has_prev = prev is not None
    mod_spec = pl.BlockSpec((1, N_MOD, tn), lambda i, j: (mod_of_tile(i), 0, j))
    specs = [pl.BlockSpec((tm, kw), lambda i, j: (i, 0)) for _ in lhs]
    specs += [pl.BlockSpec((kw, tn), functools.partial(lambda i, j, r: (r, j), r=r)) for r in range(len(lhs))]
    args = list(lhs) + [w] * len(lhs) + [x]
    specs.append(pl.BlockSpec((tm, tn), lambda i, j: (i, j)))
    if has_prev:
        args += [prev[0], prev[1]]
        specs += [pl.BlockSpec((tm, tn), lambda i, j: (i, j)), mod_spec]
    args.append(mods)
    specs.append(mod_spec)
    return pl.pallas_call(
        functools.partial(_outproj_kernel, n_lhs=len(lhs), has_prev=has_prev),
        grid=(m // tm, d // tn),
        in_specs=specs,
        out_specs=pl.BlockSpec((tm, tn), lambda i, j: (i, j)),
        out_shape=jax.ShapeDtypeStruct((m, d), F32),
        compiler_params=_params(("parallel", "parallel")),
        name="out_projection",
    )(*args)


def _ffn_prep_kernel(x_ref, mod_ref, g_ref, rw_ref, rb_ref, h_ref, gates_ref, gsteps_ref):
    h = _norm_mod(x_ref[...], g_ref[...], mod_ref[0, 3:4, :], mod_ref[0, 4:5, :])
    h_ref[...] = h.astype(h_ref.dtype)
    logits = jnp.dot(h, rw_ref[...], preferred_element_type=F32, precision=lax.Precision.HIGHEST) + rb_ref[...]
    lane = lax.broadcasted_iota(jnp.int32, logits.shape, 1)
    logits = jnp.where(lane < N_EXPERTS, logits, -jnp.inf)
    work = logits
    top = None
    denom = jnp.zeros((logits.shape[0], 1), F32)
    gates = jnp.zeros(logits.shape, F32)
    for _ in range(TOP_K):
        mx = jnp.max(work, axis=-1, keepdims=True)
        idx = jnp.min(jnp.where(work == mx, lane, LANES), axis=-1, keepdims=True)
        sel = lane == idx
        if top is None:
            top = mx
        e = jnp.exp(mx - top)
        denom = denom + e
        gates = gates + jnp.where(sel, e, 0.0)
        work = jnp.where(sel, -jnp.inf, work)
    gates = gates / denom
    gates_ref[...] = gates
    for s in range(N_EXPERT_STEPS):
        shift = (LANES - EXPERTS_PER_STEP * s) % LANES
        gsteps_ref[:, s * LANES:(s + 1) * LANES] = gates if shift == 0 else pltpu.roll(gates, shift, 1)


def ffn_prep(x, mods, g, router_w, router_b, mod_of_tile):
    m, d = x.shape
    tm = PREP_TILE
    tiles_per_row_tile = ROW_TILE // tm
    return pl.pallas_call(
        _ffn_prep_kernel,
        grid=(m // tm,),
        in_specs=[
            pl.BlockSpec((tm, d), lambda i: (i, 0)),
            pl.BlockSpec((1, N_MOD, d), lambda i: (mod_of_tile(i // tiles_per_row_tile), 0, 0)),
            pl.BlockSpec((1, d), lambda i: (0, 0)),
            pl.BlockSpec((d, LANES), lambda i: (0, 0)),
            pl.BlockSpec((1, LANES), lambda i: (0, 0)),
        ],
        out_specs=[pl.BlockSpec((tm, d), lambda i: (i, 0)),
                   pl.BlockSpec((tm, LANES), lambda i: (i, 0)),
                   pl.BlockSpec((tm, N_EXPERT_STEPS * LANES), lambda i: (i, 0))],
        out_shape=[jax.ShapeDtypeStruct((m, d), BF16),
                   jax.ShapeDtypeStruct((m, LANES), F32),
                   jax.ShapeDtypeStruct((m, N_EXPERT_STEPS * LANES), F32)],
        compiler_params=_params(("parallel",)),
        name="ffn_prep",
    )(x, mods, g.reshape(1, d), router_w, router_b)


def _moe_kernel(h_ref, gates_ref, gstep_ref, wgu_ref, bgu_ref, wdn_ref, bdn_ref, o_ref, acc_scr):
    s = pl.program_id(1)
    half = EXPERTS_PER_STEP * F_EXPERT

    @pl.when(s == 0)
    def _():
        acc_scr[...] = jnp.dot(gates_ref[...].astype(BF16), bdn_ref[...], preferred_element_type=F32)

    gu = jnp.dot(h_ref[...], wgu_ref[0], preferred_element_type=F32) + bgu_ref[0]
    glu = jnp.minimum(gu[:, :half], SWIGLU_LIMIT)
    lin = jnp.clip(gu[:, half:], -SWIGLU_LIMIT, SWIGLU_LIMIT)
    act = glu * _sigmoid(SWIGLU_ALPHA * glu) * (lin + 1.0)
    gs = gstep_ref[...]
    col = lax.broadcasted_iota(jnp.int32, act.shape, 1)
    act = act * jnp.where(col < F_EXPERT, gs[:, 0:1], gs[:, 1:2])
    acc_scr[...] += jnp.dot(act.astype(BF16), wdn_ref[0], preferred_element_type=F32)

    @pl.when(s == pl.num_programs(1) - 1)
    def _():
        o_ref[...] = acc_scr[...].astype(o_ref.dtype)


def moe_ffn(h, gates, gsteps, w_gu, b_gu, w_dn, b_dn):
    m, d = h.shape
    tm = ROW_TILE
    n_gu = 2 * EXPERTS_PER_STEP * F_EXPERT
    n_dn = EXPERTS_PER_STEP * F_EXPERT
    return pl.pallas_call(
        _moe_kernel,
        grid=(m // tm, N_EXPERT_STEPS),
        in_specs=[
            pl.BlockSpec((tm, d), lambda i, s: (i, 0)),
            pl.BlockSpec((tm, LANES), lambda i, s: (i, 0)),
            pl.BlockSpec((tm, LANES), lambda i, s: (i, s)),
            pl.BlockSpec((1, d, n_gu), lambda i, s: (s, 0, 0)),
            pl.BlockSpec((1, 1, n_gu), lambda i, s: (s, 0, 0)),
            pl.BlockSpec((1, n_dn, d), lambda i, s: (s, 0, 0)),
            pl.BlockSpec((LANES, d), lambda i, s: (0, 0)),
        ],
        out_specs=pl.BlockSpec((tm, d), lambda i, s: (i, 0)),
        out_shape=jax.ShapeDtypeStruct((m, d), BF16),
        scratch_shapes=[pltpu.VMEM((tm, d), F32)],
        compiler_params=_params(("parallel", "arbitrary")),
        name="moe_ffn",
    )(h, gates, gsteps, w_gu, b_gu, w_dn, b_dn)


def _final_kernel(x_ref, y_ref, mod_ref, g_ref, o_ref):
    x = x_ref[...] + mod_ref[0, 5:6, :] * y_ref[...].astype(F32)
    ms = jnp.mean(x * x, axis=-1, keepdims=True)
    o_ref[...] = x * lax.rsqrt(ms + NORM_EPS) * g_ref[...]


def final_norm(x, y, mods, g, n_rows, mod_of_tile):
    d = x.shape[1]
    tm = PREP_TILE
    tiles_per_row_tile = ROW_TILE // tm
    return pl.pallas_call(
        _final_kernel,
        grid=(n_rows // tm,),
        in_specs=[
            pl.BlockSpec((tm, d), lambda i: (i, 0)),
            pl.BlockSpec((tm, d), lambda i: (i, 0)),
            pl.BlockSpec((1, N_MOD, d), lambda i: (mod_of_tile(i // tiles_per_row_tile), 0, 0)),
            pl.BlockSpec((1, d), lambda i: (0, 0)),
        ],
        out_specs=pl.BlockSpec((tm, d), lambda i: (i, 0)),
        out_shape=jax.ShapeDtypeStruct((n_rows, d), F32),
        compiler_params=_params(("parallel",)),
        name="final_norm",
    )(x, y, mods, g.reshape(1, d))


def _rope_tables(seq, ctx_len):
    n_freq = HEAD_DIM // 4
    rows = seq // GRID_W
    row = jnp.repeat(jnp.arange(rows, dtype=F32), GRID_W)
    col = jnp.tile(jnp.arange(GRID_W, dtype=F32), rows)
    inv_freq = ROPE_THETA ** (-jnp.arange(n_freq, dtype=F32) / n_freq)
    ang_r = row[:, None] * inv_freq
    ang_c = col[:, None] * inv_freq
    cos = jnp.concatenate([jnp.cos(ang_r)] * 2 + [jnp.cos(ang_c)] * 2, axis=-1)
    sin = jnp.concatenate([-jnp.sin(ang_r), jnp.sin(ang_r), -jnp.sin(ang_c), jnp.sin(ang_c)], axis=-1)
    cos = jnp.concatenate([jnp.ones((ctx_len, HEAD_DIM), F32), cos], axis=0)
    sin = jnp.concatenate([jnp.zeros((ctx_len, HEAD_DIM), F32), sin], axis=0)
    return cos, sin


def _pad_lanes(v):
    v = v.reshape(1, -1).astype(F32)
    return jnp.pad(v, ((0, 0), (0, LANES - v.shape[1])))


def _expert_prep_kernel(w_ref, o_ref):
    n_in = 2 * F_EXPERT
    n = EXPERTS_PER_STEP * n_in
    w = jnp.concatenate([w_ref[e] for e in range(EXPERTS_PER_STEP)], axis=1).astype(BF16)
    src = lax.broadcasted_iota(jnp.int32, (n, n), 0)
    dst = lax.broadcasted_iota(jnp.int32, (n, n), 1)
    e = jnp.zeros((n, n), jnp.int32)
    for k in range(1, EXPERTS_PER_STEP):
        e = e + (src >= k * n_in).astype(jnp.int32)
    c = src - e * n_in
    target = (c & 1) * (EXPERTS_PER_STEP * F_EXPERT) + e * F_EXPERT + (c >> 1)
    perm = jnp.where(dst == target, 1.0, 0.0).astype(BF16)
    o_ref[0] = jnp.dot(w, perm, preferred_element_type=F32).astype(o_ref.dtype)


def expert_gate_up_prep(w_gu):
    e, d, n_in = w_gu.shape
    tk = 1024
    return pl.pallas_call(
        _expert_prep_kernel,
        grid=(e // EXPERTS_PER_STEP, d // tk),
        in_specs=[pl.BlockSpec((EXPERTS_PER_STEP, tk, n_in), lambda s, k: (s, k, 0))],
        out_specs=pl.BlockSpec((1, tk, EXPERTS_PER_STEP * n_in), lambda s, k: (s, k, 0)),
        out_shape=jax.ShapeDtypeStruct((e // EXPERTS_PER_STEP, d, EXPERTS_PER_STEP * n_in), BF16),
        compiler_params=_params(("parallel", "parallel")),
        name="expert_gate_up_prep",
    )(w_gu)


def _expert_weights(w_gu, b_gu, w_dn):
    e, d, _ = w_gu.shape
    s, p = N_EXPERT_STEPS, EXPERTS_PER_STEP
    wgu = expert_gate_up_prep(w_gu)
    bglu = b_gu[..., 0::2].reshape(s, p * F_EXPERT)
    blin = b_gu[..., 1::2].reshape(s, p * F_EXPERT)
    bgu = jnp.concatenate([bglu, blin], axis=-1).reshape(s, 1, 2 * p * F_EXPERT)
    wdn = w_dn.reshape(s, p * F_EXPERT, d).astype(BF16)
    return wgu, bgu, wdn


def kernel(x, c, ctx, c_ctx, ada_w, ada_b, norm_mix_g, norm_ffn_g, router_w, router_b, exp_w_gu, exp_b_gu,
           exp_w_down, exp_b_down, ev_w_in, ev_w_out, ev_q_g, ev_k_g, ev_conv_w, ev_conv_b, ev_a_log,
           ev_dt_bias, ev_d_skip, ev_ssm_g, od_w_in, od_w_out, od_sinks, final_g):
    batch, seq, d = x.shape
    ctx_len = ctx.shape[1]
    n_lat = batch * seq
    m = n_lat + batch * ctx_len
    assert d == D_MODEL and seq % ROW_TILE == 0 and batch * ctx_len == ROW_TILE and ctx_len == ATTN_TQ
    assert batch + 1 <= 8

    xs = jnp.concatenate([x.reshape(n_lat, d), ctx.reshape(batch * ctx_len, d)], axis=0)
    cvecs = jnp.concatenate([c, c_ctx[None, :], jnp.zeros((8 - batch - 1, d), F32)], axis=0)
    mods_all = ada_mods(cvecs, ada_w, ada_b).reshape(ada_w.shape[0], 8, N_MOD, d)[:, :batch + 1]

    tiles_per_batch = seq // ROW_TILE

    def mod_of_tile(i):
        return jnp.minimum(i // tiles_per_batch, batch)

    prep_ctx0 = n_lat // PREP_TILE
    pos_tiles = seq // PREP_TILE

    def pos_of_tile(i):
        return jnp.where(i < prep_ctx0, 1 + i % pos_tiles, 0)

    cos_t, sin_t = _rope_tables(seq, ctx_len)
    scale = HEAD_DIM ** -0.5 * LOG2_E

    prev = None
    for i in range(DEPTH):
        p = i // 2
        mods = mods_all[i]
        if i % 2 == 0:
            w_in = ev_w_in[p]
            w_main = w_in[:, :EVEN_MAIN].astype(BF16)
            w_dt = jnp.pad(w_in[:, EVEN_MAIN:], ((0, 0), (0, LANES - 2 * B_HEADS))).astype(BF16)
            yproj, dt_raw = in_projection(xs, prev, mods, norm_mix_g[i], w_main, w_dt, mod_of_tile)
            gain_q = (jnp.tile(ev_q_g[p], A_HEADS) * scale).reshape(1, -1)
            gain_k = jnp.tile(ev_k_g[p], A_KV_HEADS).reshape(1, -1)
            q_t = head_prep(yproj, 0, A_Q, (gain_q, cos_t, sin_t), pos_of_tile, use_norm=True, transpose=True)
            k = head_prep(yproj, A_Q, A_KV, (gain_k, cos_t, sin_t), pos_of_tile, use_norm=True, transpose=False)
            v_t = head_prep(yproj, EVEN_QK, A_KV, None, pos_of_tile, use_norm=False, transpose=True)
            attn = attention(q_t, k, v_t, None, batch=batch, seq=seq, ctx_len=ctx_len, n_heads=A_HEADS,
                             n_kv=A_KV_HEADS, windowed=False)
            z_col = A_Q + 2 * A_KV
            conv_w = jnp.pad(ev_conv_w[p], ((0, 8 - B_CONV_W), (0, 0)))
            xbc = conv_silu(yproj, conv_w, ev_conv_b[p].reshape(1, -1), col0=z_col + B_D_INNER, seq=seq,
                            n_lat_rows=n_lat)
            y_f, y_b = ssd_scan(xbc, dt_raw, _pad_lanes(ev_dt_bias[p]), _pad_lanes(ev_a_log[p]),
                                batch=batch, seq=seq, ctx_len=ctx_len)
            skip_row = jnp.repeat(ev_d_skip[p, 0] + ev_d_skip[p, 1], B_HEAD_DIM).reshape(1, -1)
            ssm = gated_norm(y_f, y_b, xbc, yproj, skip_row, ev_ssm_g[p].reshape(1, -1), z_col=z_col)
            xs = out_projection([attn, ssm], ev_w_out[p].astype(BF16), xs, prev, mods, mod_of_tile)
        else:
            yproj, _ = in_projection(xs, prev, mods, norm_mix_g[i], od_w_in[p].astype(BF16), None, mod_of_tile)
            gain_q = jnp.full((1, C_Q), scale, F32)
            gain_k = jnp.ones((1, C_KV), F32)
            q_t = head_prep(yproj, 0, C_Q, (gain_q, cos_t, sin_t), pos_of_tile, use_norm=False, transpose=True)
            k = head_prep(yproj, C_Q, C_KV, (gain_k, cos_t, sin_t), pos_of_tile, use_norm=False, transpose=False)
            v_t = head_prep(yproj, ODD_QK, C_KV, None, pos_of_tile, use_norm=False, transpose=True)
            attn = attention(q_t, k, v_t, od_sinks[p], batch=batch, seq=seq, ctx_len=ctx_len, n_heads=C_HEADS,
                             n_kv=C_KV_HEADS, windowed=True)
            xs = out_projection([attn], od_w_out[p].astype(BF16), xs, prev, mods, mod_of_tile)
        rw = jnp.pad(router_w[i], ((0, 0), (0, LANES - N_EXPERTS)))
        h, gates, gsteps = ffn_prep(xs, mods, norm_ffn_g[i], rw, _pad_lanes(router_b[i]), mod_of_tile)
        wgu, bgu, wdn = _expert_weights(exp_w_gu[i], exp_b_gu[i], exp_w_down[i])
        bdn = jnp.pad(exp_b_down[i], ((0, LANES - N_EXPERTS), (0, 0))).astype(BF16)
        y = moe_ffn(h, gates, gsteps, wgu, bgu, wdn, bdn)
        prev = (y, mods)

    out = final_norm(xs, prev[0], prev[1], final_g, n_lat, mod_of_tile)
    return out.reshape(batch, seq, d)
```

```python
import functools
import math

import jax
import jax.numpy as jnp
import numpy as np
from jax import lax
from jax.experimental import pallas as pl
from jax.experimental.pallas import tpu as pltpu

F32 = jnp.float32
BF16 = jnp.bfloat16

D_MODEL = 4096
HEAD_DIM = 128
GRID_W = 64
ROPE_THETA = 10000.0
NORM_EPS = 1e-6
N_MOD = 6
DEPTH = 4

A_HEADS = D_MODEL // (2 * HEAD_DIM)
A_KV_HEADS = A_HEADS // 4
A_Q = A_HEADS * HEAD_DIM
A_KV = A_KV_HEADS * HEAD_DIM

B_D_INNER = D_MODEL // 2
B_HEAD_DIM = 64
B_HEADS = B_D_INNER // B_HEAD_DIM
B_GROUPS = 4
B_HPG = B_HEADS // B_GROUPS
B_STATE = 128
B_GN = B_GROUPS * B_STATE
B_CONV_CH = B_D_INNER + 2 * B_GN
B_CONV_W = 5
B_CHUNK = 128
B_GROUP_CH = B_D_INNER // B_GROUPS

C_HEADS = D_MODEL // HEAD_DIM
C_KV_HEADS = C_HEADS // 4
C_Q = C_HEADS * HEAD_DIM
C_KV = C_KV_HEADS * HEAD_DIM
C_WINDOW = 128

N_EXPERTS = 32
TOP_K = 4
F_EXPERT = 192
SWIGLU_LIMIT = 7.0
SWIGLU_ALPHA = 1.702
F_PAD = 256
GROUP_TILE = 256
COMBINE_TILE = 128
SORT_KEY_SHIFT = 17
HALF_D = D_MODEL // 2

LANES = 128
ROW_TILE = 512
COL_TILE = 512
ATTN_TQ = 256
ATTN_TK = 512
ATTN_L_ROWS = 16
LOG2_E = math.log2(math.e)
PREP_TILE = 256
VMEM_LIMIT = 56 * 1024 * 1024

EVEN_MAIN = A_Q + 2 * A_KV + B_D_INNER + B_CONV_CH
EVEN_QK = A_Q + A_KV
ODD_IN = C_Q + 2 * C_KV
ODD_QK = C_Q + C_KV


def _params(sem):
    return pltpu.CompilerParams(dimension_semantics=sem, vmem_limit_bytes=VMEM_LIMIT)


def _sigmoid(v):
    return 1.0 / (1.0 + jnp.exp(-v))


def _softplus(v):
    return jnp.maximum(v, 0.0) + jnp.log(1.0 + jnp.exp(-jnp.abs(v)))


def _ada_kernel(c_ref, w_ref, b_ref, o_ref):
    c = c_ref[...]
    s = (c * _sigmoid(c)).astype(BF16)
    o_ref[0] = jnp.dot(s, w_ref[0].astype(BF16), preferred_element_type=F32) + b_ref[0]


def ada_mods(cvecs, ada_w, ada_b):
    depth, d, n = ada_w.shape
    tn = COL_TILE
    return pl.pallas_call(
        _ada_kernel,
        grid=(depth, n // tn),
        in_specs=[
            pl.BlockSpec((8, d), lambda l, j: (0, 0)),
            pl.BlockSpec((1, d, tn), lambda l, j: (l, 0, j)),
            pl.BlockSpec((1, 1, tn), lambda l, j: (l, 0, j)),
        ],
        out_specs=pl.BlockSpec((1, 8, tn), lambda l, j: (l, 0, j)),
        out_shape=jax.ShapeDtypeStruct((depth, 8, n), F32),
        compiler_params=_params(("parallel", "parallel")),
        name="ada_mods",
    )(cvecs, ada_w, ada_b.reshape(depth, 1, n))


def _norm_mod(x, g, shift, scale):
    ms = jnp.mean(x * x, axis=-1, keepdims=True)
    return (x * lax.rsqrt(ms + NORM_EPS) * g) * (1.0 + scale) + shift


def _inproj_kernel(*refs, has_prev, has_dt):
    it = iter(refs)
    x_ref = next(it)
    if has_prev:
        y_ref = next(it)
        pmod_ref = next(it)
    mod_ref = next(it)
    g_ref = next(it)
    w_ref = next(it)
    if has_dt:
        wdt_ref = next(it)
    o_ref = next(it)
    if has_dt:
        dt_ref = next(it)
    h_scr = next(it)

    @pl.when(pl.program_id(1) == 0)
    def _():
        x = x_ref[...]
        if has_prev:
            x = x + pmod_ref[0, 5:6, :] * y_ref[...].astype(F32)
        h = _norm_mod(x, g_ref[...], mod_ref[0, 0:1, :], mod_ref[0, 1:2, :])
        h_scr[...] = h.astype(BF16)
        if has_dt:
            dt_ref[...] = jnp.dot(h_scr[...], wdt_ref[...], preferred_element_type=F32)

    o_ref[...] = jnp.dot(h_scr[...], w_ref[...], preferred_element_type=F32).astype(o_ref.dtype)


def in_projection(x, prev, mods, g, w, w_dt, mod_of_tile):
    m, d = x.shape
    n = w.shape[1]
    tm, tn = ROW_TILE, COL_TILE
    has_prev = prev is not None
    has_dt = w_dt is not None
    mod_spec = pl.BlockSpec((1, N_MOD, d), lambda i, j: (mod_of_tile(i), 0, 0))
    args = [x]
    specs = [pl.BlockSpec((tm, d), lambda i, j: (i, 0))]
    if has_prev:
        args += [prev[0], prev[1]]
        specs += [pl.BlockSpec((tm, d), lambda i, j: (i, 0)), mod_spec]
    args += [mods, g.reshape(1, d), w]
    specs += [mod_spec, pl.BlockSpec((1, d), lambda i, j: (0, 0)), pl.BlockSpec((d, tn), lambda i, j: (0, j))]
    out_shape = [jax.ShapeDtypeStruct((m, n), BF16)]
    out_specs = [pl.BlockSpec((tm, tn), lambda i, j: (i, j))]
    if has_dt:
        args.append(w_dt)
        specs.append(pl.BlockSpec((d, LANES), lambda i, j: (0, 0)))
        out_shape.append(jax.ShapeDtypeStruct((m, LANES), F32))
        out_specs.append(pl.BlockSpec((tm, LANES), lambda i, j: (i, 0)))
    res = pl.pallas_call(
        functools.partial(_inproj_kernel, has_prev=has_prev, has_dt=has_dt),
        grid=(m // tm, n // tn),
        in_specs=specs,
        out_specs=out_specs,
        out_shape=out_shape,
        scratch_shapes=[pltpu.VMEM((tm, d), BF16)],
        compiler_params=_params(("parallel", "arbitrary")),
        name="in_projection",
    )(*args)
    return res if has_dt else (res[0], None)


def _head_prep_kernel(*refs, use_norm, use_rope, transpose):
    y_ref = refs[0]
    o_ref = refs[-1]
    if use_rope:
        gain_ref, cos_ref, sin_ref = refs[1:4]
        cos = cos_ref[...]
        sin = sin_ref[...]
        lane = lax.broadcasted_iota(jnp.int32, cos.shape, 1)
        first_half = (lane & (HEAD_DIM // 4)) == 0
    for hh in range(COL_TILE // HEAD_DIM):
        sl = slice(hh * HEAD_DIM, (hh + 1) * HEAD_DIM)
        v = y_ref[:, sl].astype(F32)
        if use_norm:
            v = v * lax.rsqrt(jnp.mean(v * v, axis=-1, keepdims=True) + NORM_EPS)
        if use_rope:
            v = v * gain_ref[:, sl]
            swapped = jnp.where(first_half, pltpu.roll(v, HEAD_DIM - HEAD_DIM // 4, 1),
                                pltpu.roll(v, HEAD_DIM // 4, 1))
            v = v * cos + swapped * sin
        if transpose:
            o_ref[sl, :] = v.T.astype(o_ref.dtype)
        else:
            o_ref[:, sl] = v.astype(o_ref.dtype)


def head_prep(yproj, col0, n_cols, rope, pos_of_tile, *, use_norm, transpose):
    m = yproj.shape[0]
    tm = PREP_TILE
    c0 = col0 // COL_TILE
    use_rope = rope is not None
    args = [yproj]
    specs = [pl.BlockSpec((tm, COL_TILE), lambda i, j: (i, c0 + j))]
    if use_rope:
        args += list(rope)
        specs += [pl.BlockSpec((1, COL_TILE), lambda i, j: (0, j)),
                  pl.BlockSpec((tm, HEAD_DIM), lambda i, j: (pos_of_tile(i), 0)),
                  pl.BlockSpec((tm, HEAD_DIM), lambda i, j: (pos_of_tile(i), 0))]
    if transpose:
        out_spec = pl.BlockSpec((COL_TILE, tm), lambda i, j: (j, i))
        out_shape = jax.ShapeDtypeStruct((n_cols, m), BF16)
    else:
        out_spec = pl.BlockSpec((tm, COL_TILE), lambda i, j: (i, j))
        out_shape = jax.ShapeDtypeStruct((m, n_cols), BF16)
    return pl.pallas_call(
        functools.partial(_head_prep_kernel, use_norm=use_norm, use_rope=use_rope, transpose=transpose),
        grid=(m // tm, n_cols // COL_TILE),
        in_specs=specs,
        out_specs=out_spec,
        out_shape=out_shape,
        compiler_params=_params(("parallel", "parallel")),
        name="head_prep",
    )(*args)


def _attn_kernel(*refs, group, n_lat_tiles, seq, windowed, has_sink):
    it = iter(refs)
    if has_sink:
        sink_ref = next(it)
    qt_ref, kc_ref, vtc_ref, kl_ref, vtl_ref, o_ref, m_scr, acc_scr = (next(it) for _ in range(8))
    tq = ATTN_TQ
    kvh = pl.program_id(1)
    qi = pl.program_id(2)
    cols = group * tq

    q_t = jnp.concatenate([qt_ref[g * HEAD_DIM:(g + 1) * HEAD_DIM, :] for g in range(group)], axis=1)

    if has_sink:
        m_scr[...] = jnp.concatenate(
            [jnp.full((1, tq), sink_ref[kvh * group + g] * LOG2_E, F32) for g in range(group)], axis=1)
        acc_scr[...] = jnp.concatenate([jnp.zeros((HEAD_DIM, cols), F32), jnp.ones((ATTN_L_ROWS, cols), F32)], axis=0)
    else:
        m_scr[...] = jnp.full((1, cols), -jnp.inf, F32)
        acc_scr[...] = jnp.zeros((HEAD_DIM + ATTN_L_ROWS, cols), F32)

    def step(k, v_t, mask):
        s = jnp.dot(k, q_t, preferred_element_type=F32)
        if mask is not None:
            s = jnp.where(mask, s, -jnp.inf)
        m_prev = m_scr[...]
        m_new = jnp.maximum(m_prev, jnp.max(s, axis=0, keepdims=True))
        alpha = jnp.exp2(m_prev - m_new)
        p = jnp.exp2(s - m_new).astype(BF16)
        v_ext = jnp.concatenate([v_t, jnp.ones((ATTN_L_ROWS, v_t.shape[1]), BF16)], axis=0)
        acc_scr[...] = alpha * acc_scr[...] + jnp.dot(v_ext, p, preferred_element_type=F32)
        m_scr[...] = m_new

    step(kc_ref[...], vtc_ref[...], None)

    @pl.when(qi < n_lat_tiles)
    def _():
        if windowed:
            band = tq + 2 * C_WINDOW
            start = jnp.clip(qi * tq - C_WINDOW, 0, seq - band)
            start = pl.multiple_of(start, C_WINDOW)
            k_pos = start + lax.broadcasted_iota(jnp.int32, (band, cols), 0)
            q_pos = qi * tq + (lax.broadcasted_iota(jnp.int32, (band, cols), 1) & (tq - 1))
            mask = jnp.abs(k_pos - q_pos) <= C_WINDOW
            step(kl_ref[pl.ds(start, band), :], vtl_ref[:, pl.ds(start, band)], mask)
        else:
            def body(c, carry):
                start = pl.multiple_of(c * ATTN_TK, ATTN_TK)
                step(kl_ref[pl.ds(start, ATTN_TK), :], vtl_ref[:, pl.ds(start, ATTN_TK)], None)
                return carry
            lax.fori_loop(0, seq // ATTN_TK, body, 0, unroll=2)

    out_t = acc_scr[0:HEAD_DIM, :] / acc_scr[HEAD_DIM:HEAD_DIM + 1, :]
    for g in range(group):
        o_ref[:, g * HEAD_DIM:(g + 1) * HEAD_DIM] = out_t[:, g * tq:(g + 1) * tq].T.astype(o_ref.dtype)


def attention(q_t, k, v_t, sinks, *, batch, seq, ctx_len, n_heads, n_kv, windowed):
    m = k.shape[0]
    group = n_heads // n_kv
    tq = ATTN_TQ
    assert ctx_len == tq and seq % ATTN_TK == 0 and seq >= tq + 2 * C_WINDOW and tq & (tq - 1) == 0
    n_lat_tiles = seq // tq
    ctx_block0 = batch * seq // ctx_len
    has_sink = sinks is not None

    def q_tile(b, i):
        return jnp.where(i < n_lat_tiles, b * n_lat_tiles + i, batch * n_lat_tiles + b)

    in_specs = [
        pl.BlockSpec((group * HEAD_DIM, tq), lambda b, h, i, *_: (h, q_tile(b, i))),
        pl.BlockSpec((ctx_len, HEAD_DIM), lambda b, h, i, *_: (ctx_block0 + b, h)),
        pl.BlockSpec((HEAD_DIM, ctx_len), lambda b, h, i, *_: (h, ctx_block0 + b)),
        pl.BlockSpec((seq, HEAD_DIM), lambda b, h, i, *_: (b, h)),
        pl.BlockSpec((HEAD_DIM, seq), lambda b, h, i, *_: (h, b)),
    ]
    args = [q_t, k, v_t, k, v_t]
    cols = group * tq
    grid_spec = pltpu.PrefetchScalarGridSpec(
        num_scalar_prefetch=1 if has_sink else 0,
        grid=(batch, n_kv, n_lat_tiles + 1),
        in_specs=in_specs,
        out_specs=pl.BlockSpec((tq, group * HEAD_DIM), lambda b, h, i, *_: (q_tile(b, i), h)),
        scratch_shapes=[pltpu.VMEM((1, cols), F32), pltpu.VMEM((HEAD_DIM + ATTN_L_ROWS, cols), F32)],
    )
    if has_sink:
        args = [sinks] + args
    return pl.pallas_call(
        functools.partial(_attn_kernel, group=group, n_lat_tiles=n_lat_tiles, seq=seq,
                          windowed=windowed, has_sink=has_sink),
        grid_spec=grid_spec,
        out_shape=jax.ShapeDtypeStruct((m, n_heads * HEAD_DIM), BF16),
        compiler_params=_params(("parallel", "parallel", "arbitrary")),
        name="attention_window" if windowed else "attention_global",
    )(*args)


CONV_HALO = 16


def _conv_kernel(prev_ref, cur_ref, next_ref, w_ref, b_ref, o_ref, *, tiles_per_seq, n_lat_tiles):
    i = pl.program_id(0)
    tm = cur_ref.shape[0]
    in_lat = i < n_lat_tiles
    pos = i % tiles_per_seq
    has_prev = jnp.logical_and(in_lat, pos != 0)
    has_next = jnp.logical_and(in_lat, pos != tiles_per_seq - 1)
    prev = jnp.where(has_prev, prev_ref[...].astype(F32), 0.0)
    nxt = jnp.where(has_next, next_ref[...].astype(F32), 0.0)
    ext = jnp.concatenate([prev, cur_ref[...].astype(F32), nxt], axis=0)
    n_ext = tm + 2 * CONV_HALO
    acc = jnp.zeros((tm, ext.shape[1]), F32) + b_ref[...]
    for k in range(B_CONV_W):
        shift = (B_CONV_W // 2 - k) % n_ext
        shifted = ext if shift == 0 else pltpu.roll(ext, shift, 0)
        acc = acc + shifted[CONV_HALO:CONV_HALO + tm, :] * w_ref[k:k + 1, :]
    o_ref[...] = (acc * _sigmoid(acc)).astype(o_ref.dtype)


def conv_silu(yproj, conv_w, conv_b, *, col0, seq, n_lat_rows):
    m = yproj.shape[0]
    tm = PREP_TILE
    tiles_per_seq = seq // tm
    n_lat_tiles = n_lat_rows // tm
    c0 = col0 // COL_TILE
    halo_per_tile = tm // CONV_HALO
    n_halo_blocks = m // CONV_HALO
    return pl.pallas_call(
        functools.partial(_conv_kernel, tiles_per_seq=tiles_per_seq, n_lat_tiles=n_lat_tiles),
        grid=(m // tm, B_CONV_CH // COL_TILE),
        in_specs=[
            pl.BlockSpec((CONV_HALO, COL_TILE), lambda i, j: (jnp.maximum(i * halo_per_tile - 1, 0), c0 + j)),
            pl.BlockSpec((tm, COL_TILE), lambda i, j: (i, c0 + j)),
            pl.BlockSpec((CONV_HALO, COL_TILE),
                         lambda i, j: (jnp.minimum((i + 1) * halo_per_tile, n_halo_blocks - 1), c0 + j)),
            pl.BlockSpec((8, COL_TILE), lambda i, j: (0, j)),
            pl.BlockSpec((1, COL_TILE), lambda i, j: (0, j)),
        ],
        out_specs=pl.BlockSpec((tm, COL_TILE), lambda i, j: (i, j)),
        out_shape=jax.ShapeDtypeStruct((m, B_CONV_CH), BF16),
        compiler_params=_params(("parallel", "parallel")),
        name="conv_silu",
    )(yproj, yproj, yproj, conv_w, conv_b)


def _ssd_kernel(xf_ref, bf_ref, cf_ref, dtf_ref, xb_ref, bb_ref, cb_ref, dtb_ref, bias_ref, alog_ref,
                yf_ref, yb_ref, h_scr):
    t = pl.program_id(1)
    q = B_CHUNK

    @pl.when(t == 0)
    def _():
        h_scr[...] = jnp.zeros(h_scr.shape, F32)

    row = lax.broadcasted_iota(jnp.int32, (q, q), 0)
    col = lax.broadcasted_iota(jnp.int32, (q, q), 1)
    lane_lo = lax.broadcasted_iota(jnp.int32, (q, LANES), 1) < B_HEAD_DIM
    lane_lo_row = lane_lo[0:1, :]
    a_coef = -jnp.exp(alog_ref[...])

    dirs = ((xf_ref, bf_ref, cf_ref, dtf_ref, yf_ref, col <= row, q - 1),
            (xb_ref, bb_ref, cb_ref, dtb_ref, yb_ref, col >= row, 0))
    for d, (x_ref, b_ref, c_ref, dt_ref, y_ref, tri, last_row) in enumerate(dirs):
        dt = _softplus(dt_ref[...] + bias_ref[...])
        ld = dt * a_coef
        cum = jnp.dot(tri.astype(F32), ld, preferred_element_type=F32, precision=lax.Precision.HIGHEST)
        cum_t = cum.T
        last = cum[last_row:last_row + 1, :]
        e_cum = jnp.exp(cum)
        e_rest = jnp.exp(last - cum)
        e_last = jnp.exp(last)

        def pick(arr, l0, lo=lane_lo):
            return jnp.where(lo, arr[:, l0:l0 + 1], arr[:, l0 + 1:l0 + 2])

        for g in range(B_GROUPS):
            bm = b_ref[:, g * B_STATE:(g + 1) * B_STATE]
            cm = c_ref[:, g * B_STATE:(g + 1) * B_STATE]
            cbm = lax.dot_general(cm, bm, (((1,), (1,)), ((), ())), preferred_element_type=F32)
            bm_t = bm.astype(F32).T.astype(BF16)
            h_prev = h_scr[d, g]
            y_off = jnp.dot(cm, h_prev.astype(BF16), preferred_element_type=F32)
            xw_parts = []
            dec_parts = []
            for p in range(B_HPG // 2):
                l0 = d * (B_GROUPS * B_HPG) + g * B_HPG + 2 * p
                c0 = g * B_GROUP_CH + p * LANES
                xdt = x_ref[:, c0:c0 + LANES].astype(F32) * pick(dt, l0)
                y_p = y_off[:, p * LANES:(p + 1) * LANES] * pick(e_cum, l0)
                for half in range(2):
                    l = l0 + half
                    seg = jnp.exp(jnp.where(tri, cum[:, l:l + 1] - cum_t[l:l + 1, :], -jnp.inf))
                    keep = lane_lo if half == 0 else jnp.logical_not(lane_lo)
                    y_p = y_p + jnp.dot((cbm * seg).astype(BF16), jnp.where(keep, xdt, 0.0).astype(BF16),
                                        preferred_element_type=F32)
                y_ref[:, c0:c0 + LANES] = y_p.astype(y_ref.dtype)
                xw_parts.append((xdt * pick(e_rest, l0)).astype(BF16))
                dec_parts.append(pick(e_last, l0, lane_lo_row))
            xw = jnp.concatenate(xw_parts, axis=1)
            dec = jnp.concatenate(dec_parts, axis=1)
            h_scr[d, g] = h_prev * dec + jnp.dot(bm_t, xw, preferred_element_type=F32)


def ssd_scan(xbc, dt_raw, dt_bias_row, a_log_row, *, batch, seq, ctx_len):
    m = xbc.shape[0]
    q = B_CHUNK
    nc_lat, nc_ctx = seq // q, ctx_len // q
    n_steps = nc_ctx + nc_lat
    ctx0 = batch * nc_lat

    def fwd_blk(b, t):
        return jnp.where(t < nc_ctx, ctx0 + b * nc_ctx + t, b * nc_lat + t - nc_ctx)

    def bwd_blk(b, t):
        return jnp.where(t < nc_ctx, ctx0 + b * nc_ctx + (nc_ctx - 1 - t), b * nc_lat + (n_steps - 1 - t))

    xb, bb, cb = 0, B_D_INNER // B_GN, B_D_INNER // B_GN + 1

    def specs(blk):
        return [
            pl.BlockSpec((q, B_D_INNER), lambda b, t: (blk(b, t), xb)),
            pl.BlockSpec((q, B_GN), lambda b, t: (blk(b, t), bb)),
            pl.BlockSpec((q, B_GN), lambda b, t: (blk(b, t), cb)),
            pl.BlockSpec((q, LANES), lambda b, t: (blk(b, t), 0)),
        ]

    row_spec = pl.BlockSpec((1, LANES), lambda b, t: (0, 0))
    return pl.pallas_call(
        _ssd_kernel,
        grid=(batch, n_steps),
        in_specs=specs(fwd_blk) + specs(bwd_blk) + [row_spec, row_spec],
        out_specs=[pl.BlockSpec((q, B_D_INNER), lambda b, t: (fwd_blk(b, t), 0)),
                   pl.BlockSpec((q, B_D_INNER), lambda b, t: (bwd_blk(b, t), 0))],
        out_shape=[jax.ShapeDtypeStruct((m, B_D_INNER), BF16)] * 2,
        scratch_shapes=[pltpu.VMEM((2, B_GROUPS, B_STATE, B_GROUP_CH), F32)],
        compiler_params=_params(("parallel", "arbitrary")),
        name="ssd_scan",
    )(xbc, xbc, xbc, dt_raw, xbc, xbc, xbc, dt_raw, dt_bias_row, a_log_row)


def _gated_norm_kernel(yf_ref, yb_ref, x_ref, z_ref, skip_ref, g_ref, o_ref):
    z = z_ref[...].astype(F32)
    y = yf_ref[...].astype(F32) + yb_ref[...].astype(F32) + skip_ref[...] * x_ref[...].astype(F32)
    yz = y * (z * _sigmoid(z))
    ms = jnp.mean(yz * yz, axis=-1, keepdims=True)
    o_ref[...] = (yz * lax.rsqrt(ms + NORM_EPS) * g_ref[...]).astype(o_ref.dtype)


def gated_norm(y_f, y_b, xbc, yproj, skip_row, g_row, *, z_col):
    m = y_f.shape[0]
    tm = PREP_TILE
    w = B_GROUP_CH
    z0 = z_col // w
    blk = pl.BlockSpec((tm, w), lambda i, j: (i, j))
    row = pl.BlockSpec((1, w), lambda i, j: (0, j))
    return pl.pallas_call(
        _gated_norm_kernel,
        grid=(m // tm, B_GROUPS),
        in_specs=[blk, blk, blk, pl.BlockSpec((tm, w), lambda i, j: (i, z0 + j)), row, row],
        out_specs=blk,
        out_shape=jax.ShapeDtypeStruct((m, B_D_INNER), BF16),
        compiler_params=_params(("parallel", "parallel")),
        name="gated_norm",
    )(y_f, y_b, xbc, yproj, skip_row, g_row)


def _outproj_kernel(*refs, n_lhs, has_prev):
    it = iter(refs)
    lhs = [next(it) for _ in range(n_lhs)]
    ws = [next(it) for _ in range(n_lhs)]
    x_ref = next(it)
    if has_prev:
        y_ref = next(it)
        pmod_ref = next(it)
    mod_ref = next(it)
    o_ref = next(it)
    acc = jnp.dot(lhs[0][...], ws[0][...], preferred_element_type=F32)
    for a, w in zip(lhs[1:], ws[1:]):
        acc = acc + jnp.dot(a[...], w[...], preferred_element_type=F32)
    x = x_ref[...]
    if has_prev:
        x = x + pmod_ref[0, 5:6, :] * y_ref[...].astype(F32)
    o_ref[...] = x + mod_ref[0, 2:3, :] * acc


def out_projection(lhs, w, x, prev, mods, mod_of_tile):
    m, d = x.shape
    tm, tn = ROW_TILE, COL_TILE
    kw = lhs[0].shape[1]
    has_prev = prev is not None
    mod_spec = pl.BlockSpec((1, N_MOD, tn), lambda i, j: (mod_of_tile(i), 0, j))
    specs = [pl.BlockSpec((tm, kw), lambda i, j: (i, 0)) for _ in lhs]
    specs += [pl.BlockSpec((kw, tn), functools.partial(lambda i, j, r: (r, j), r=r)) for r in range(len(lhs))]
    args = list(lhs) + [w] * len(lhs) + [x]
    specs.append(pl.BlockSpec((tm, tn), lambda i, j: (i, j)))
    if has_prev:
        args += [prev[0], prev[1]]
        specs += [pl.BlockSpec((tm, tn), lambda i, j: (i, j)), mod_spec]
    args.append(mods)
    specs.append(mod_spec)
    return pl.pallas_call(
        functools.partial(_outproj_kernel, n_lhs=len(lhs), has_prev=has_prev),
        grid=(m // tm, d // tn),
        in_specs=specs,
        out_specs=pl.BlockSpec((tm, tn), lambda i, j: (i, j)),
        out_shape=jax.ShapeDtypeStruct((m, d), F32),
        compiler_params=_params(("parallel", "parallel")),
        name="out_projection",
    )(*args)


def _gather_params(sem):
    return pltpu.CompilerParams(dimension_semantics=sem, vmem_limit_bytes=VMEM_LIMIT, disable_bounds_checks=True)


def _pack_bf16_pairs(v):
    bits = lax.bitcast_convert_type(v.astype(BF16).astype(F32), jnp.uint32)
    return bits[:, HALF_D:] | (bits[:, :HALF_D] >> 16)


def _unpack_bf16_pairs(w):
    lo = lax.bitcast_convert_type(w << 16, F32)
    hi = lax.bitcast_convert_type(w & jnp.uint32(0xFFFF0000), F32)
    return lo, hi


def _ffn_prep_kernel(x_ref, mod_ref, g_ref, rw_ref, rb_ref, hp_ref, topi_ref, topw_ref):
    h = _norm_mod(x_ref[...], g_ref[...], mod_ref[0, 3:4, :], mod_ref[0, 4:5, :])
    hp_ref[...] = _pack_bf16_pairs(h)
    logits = jnp.dot(h, rw_ref[...], preferred_element_type=F32, precision=lax.Precision.HIGHEST) + rb_ref[...]
    lane = lax.broadcasted_iota(jnp.int32, logits.shape, 1)
    work = jnp.where(lane < N_EXPERTS, logits, -jnp.inf)
    top = None
    denom = jnp.zeros((logits.shape[0], 1), F32)
    topi = jnp.zeros(logits.shape, jnp.int32)
    topw = jnp.zeros(logits.shape, F32)
    for k in range(TOP_K):
        mx = jnp.max(work, axis=-1, keepdims=True)
        idx = jnp.min(jnp.where(work == mx, lane, LANES), axis=-1, keepdims=True)
        if top is None:
            top = mx
        e = jnp.exp(mx - top)
        denom = denom + e
        topi = jnp.where(lane == k, idx, topi)
        topw = jnp.where(lane == k, e, topw)
        work = jnp.where(lane == idx, -jnp.inf, work)
    topi_ref[...] = topi
    topw_ref[...] = topw / denom


def ffn_prep(x, mods, g, router_w, router_b, mod_of_tile):
    m, d = x.shape
    tm = PREP_TILE
    tiles_per_row_tile = ROW_TILE // tm
    return pl.pallas_call(
        _ffn_prep_kernel,
        grid=(m // tm,),
        in_specs=[
            pl.BlockSpec((tm, d), lambda i: (i, 0)),
            pl.BlockSpec((1, N_MOD, d), lambda i: (mod_of_tile(i // tiles_per_row_tile), 0, 0)),
            pl.BlockSpec((1, d), lambda i: (0, 0)),
            pl.BlockSpec((d, LANES), lambda i: (0, 0)),
            pl.BlockSpec((1, LANES), lambda i: (0, 0)),
        ],
        out_specs=[pl.BlockSpec((tm, d // 2), lambda i: (i, 0)),
                   pl.BlockSpec((tm, LANES), lambda i: (i, 0)),
                   pl.BlockSpec((tm, LANES), lambda i: (i, 0))],
        out_shape=[jax.ShapeDtypeStruct((m, d // 2), jnp.uint32),
                   jax.ShapeDtypeStruct((m, LANES), jnp.int32),
                   jax.ShapeDtypeStruct((m, LANES), F32)],
        compiler_params=_params(("parallel",)),
        name="ffn_prep",
    )(x, mods, g.reshape(1, d), router_w, router_b)


def routing_tables(topi):
    m = topi.shape[0]
    ids = topi[:, :TOP_K].reshape(-1)
    n_assign = ids.shape[0]
    assert n_assign < (1 << SORT_KEY_SHIFT)
    tg = GROUP_TILE
    n_tiles = (n_assign + N_EXPERTS * tg) // tg
    onehot = (ids[:, None] == jnp.arange(N_EXPERTS, dtype=jnp.int32)[None, :]).astype(jnp.int32)
    csum = jnp.cumsum(onehot, axis=0)
    counts = csum[-1]
    rank = jnp.take_along_axis(csum, ids[:, None], axis=1)[:, 0] - 1
    padded = ((counts + tg - 1) // tg) * tg
    g_end = jnp.cumsum(padded)
    g_start = g_end - padded
    u_start = jnp.cumsum(counts) - counts
    slot = g_start[ids] + rank
    keys = jnp.sort(ids * (1 << SORT_KEY_SHIFT) + jnp.arange(n_assign, dtype=jnp.int32))
    order = keys & ((1 << SORT_KEY_SHIFT) - 1)
    tile_start = jnp.arange(n_tiles, dtype=jnp.int32) * tg
    tile_e = jnp.sum((tile_start[:, None] >= g_end[None, :]).astype(jnp.int32), axis=1)
    n_valid = (g_end[-1] // tg).reshape(1)
    tile_ec = jnp.minimum(tile_e, N_EXPERTS - 1)
    s = jnp.arange(n_tiles * tg, dtype=jnp.int32)
    e_s = jnp.repeat(tile_ec, tg)
    off = s - g_start[e_s]
    valid = jnp.logical_and(off < counts[e_s], jnp.repeat(tile_e, tg) < N_EXPERTS)
    r = jnp.clip(u_start[e_s] + off, 0, n_assign - 1)
    src_token = jnp.where(valid, order[r] // TOP_K, 0).reshape(n_tiles, 1, tg)
    tc = COMBINE_TILE
    slot_of = slot.reshape(m // tc, tc, TOP_K).transpose(0, 2, 1).reshape(m // tc, 1, TOP_K * tc)
    return tile_ec, n_valid, src_token, slot_of


def _row_copy(src_hbm, src_row, dst, dst_row, sem):
    return pltpu.make_async_copy(src_hbm.at[pl.ds(src_row, 1)], dst.at[pl.ds(dst_row, 1)], sem)


def _start_row_gather(idx_ref, n_rows, src_hbm, dst, sem):
    def body(r, carry):
        _row_copy(src_hbm, idx_ref[0, 0, r], dst, r, sem).start()
        return carry
    lax.fori_loop(0, n_rows, body, 0, unroll=8)


def _wait_row_gather(n_rows, src_hbm, dst, sem):
    def body(r, carry):
        _row_copy(src_hbm, 0, dst, r, sem).wait()
        return carry
    lax.fori_loop(0, n_rows, body, 0, unroll=8)


def _expert_kernel(tile_e_ref, n_valid_ref, idx_ref, idx_next_ref, h_hbm, wgu_ref, bgu_ref, wdn_ref, bdn_ref,
                   o_ref, hbuf, sem):
    t = pl.program_id(0)
    n_valid = n_valid_ref[0]
    cur = t % 2

    @pl.when(t == 0)
    def _():
        _start_row_gather(idx_ref, GROUP_TILE, h_hbm, hbuf.at[0], sem.at[0])

    @pl.when(t + 1 < n_valid)
    def _():
        _start_row_gather(idx_next_ref, GROUP_TILE, h_hbm, hbuf.at[1 - cur], sem.at[1 - cur])

    @pl.when(t < n_valid)
    def _():
        _wait_row_gather(GROUP_TILE, h_hbm, hbuf.at[cur], sem.at[cur])
        lo, hi = _unpack_bf16_pairs(hbuf[cur])
        h = jnp.concatenate([lo.astype(BF16), hi.astype(BF16)], axis=1)
        gu = jnp.dot(h, wgu_ref[0], preferred_element_type=F32) + bgu_ref[0]
        glu = jnp.minimum(gu[:, :F_PAD], SWIGLU_LIMIT)
        lin = jnp.clip(gu[:, F_PAD:], -SWIGLU_LIMIT, SWIGLU_LIMIT)
        act = glu * _sigmoid(SWIGLU_ALPHA * glu) * (lin + 1.0)
        y = jnp.dot(act.astype(BF16), wdn_ref[0], preferred_element_type=F32) + bdn_ref[0]
        o_ref[...] = _pack_bf16_pairs(y)

    @pl.when(t >= n_valid)
    def _():
        o_ref[...] = jnp.zeros(o_ref.shape, o_ref.dtype)


def expert_sweep(h_packed, tile_e, n_valid, src_token, wgu, bgu, wdn, bdn):
    n_tiles = src_token.shape[0]
    tg = GROUP_TILE
    d = D_MODEL
    grid_spec = pltpu.PrefetchScalarGridSpec(
        num_scalar_prefetch=2,
        grid=(n_tiles,),
        in_specs=[
            pl.BlockSpec((1, 1, tg), lambda t, te, nv: (t, 0, 0), memory_space=pltpu.SMEM),
            pl.BlockSpec((1, 1, tg), lambda t, te, nv: (jnp.minimum(t + 1, n_tiles - 1), 0, 0),
                         memory_space=pltpu.SMEM),
            pl.BlockSpec(memory_space=pl.ANY),
            pl.BlockSpec((1, d, 2 * F_PAD), lambda t, te, nv: (te[t], 0, 0)),
            pl.BlockSpec((1, 1, 2 * F_PAD), lambda t, te, nv: (te[t], 0, 0)),
            pl.BlockSpec((1, F_PAD, d), lambda t, te, nv: (te[t], 0, 0)),
            pl.BlockSpec((1, 1, d), lambda t, te, nv: (te[t], 0, 0)),
        ],
        out_specs=pl.BlockSpec((tg, d // 2), lambda t, te, nv: (t, 0)),
        scratch_shapes=[pltpu.VMEM((2, tg, d // 2), jnp.uint32), pltpu.SemaphoreType.DMA((2,))],
    )
    return pl.pallas_call(
        _expert_kernel,
        grid_spec=grid_spec,
        out_shape=jax.ShapeDtypeStruct((n_tiles * tg, d // 2), jnp.uint32),
        compiler_params=_gather_params(("arbitrary",)),
        name="expert_sweep",
    )(tile_e, n_valid, src_token, src_token, h_packed, wgu, bgu, wdn, bdn)


def _combine_kernel(slot_ref, slot_next_ref, ys_hbm, topw_ref, o_ref, buf, sem):
    i = pl.program_id(0)
    n = pl.num_programs(0)
    cur = i % 2
    n_rows = TOP_K * COMBINE_TILE

    @pl.when(i == 0)
    def _():
        _start_row_gather(slot_ref, n_rows, ys_hbm, buf.at[0], sem.at[0])

    @pl.when(i + 1 < n)
    def _():
        _start_row_gather(slot_next_ref, n_rows, ys_hbm, buf.at[1 - cur], sem.at[1 - cur])

    _wait_row_gather(n_rows, ys_hbm, buf.at[cur], sem.at[cur])
    w = topw_ref[...]
    acc_lo = jnp.zeros((COMBINE_TILE, HALF_D), F32)
    acc_hi = jnp.zeros((COMBINE_TILE, HALF_D), F32)
    for k in range(TOP_K):
        lo, hi = _unpack_bf16_pairs(buf[cur, k * COMBINE_TILE:(k + 1) * COMBINE_TILE, :])
        acc_lo = acc_lo + w[:, k:k + 1] * lo
        acc_hi = acc_hi + w[:, k:k + 1] * hi
    o_ref[:, :HALF_D] = acc_lo.astype(o_ref.dtype)
    o_ref[:, HALF_D:] = acc_hi.astype(o_ref.dtype)


def moe_combine(ys_packed, slot_of, topw):
    m = topw.shape[0]
    tc = COMBINE_TILE
    n = m // tc
    d = D_MODEL
    return pl.pallas_call(
        _combine_kernel,
        grid=(n,),
        in_specs=[
            pl.BlockSpec((1, 1, TOP_K * tc), lambda i: (i, 0, 0), memory_space=pltpu.SMEM),
            pl.BlockSpec((1, 1, TOP_K * tc), lambda i: (jnp.minimum(i + 1, n - 1), 0, 0), memory_space=pltpu.SMEM),
            pl.BlockSpec(memory_space=pl.ANY),
            pl.BlockSpec((tc, LANES), lambda i: (i, 0)),
        ],
        out_specs=pl.BlockSpec((tc, d), lambda i: (i, 0)),
        out_shape=jax.ShapeDtypeStruct((m, d), BF16),
        scratch_shapes=[pltpu.VMEM((2, TOP_K * tc, d // 2), jnp.uint32), pltpu.SemaphoreType.DMA((2,))],
        compiler_params=_gather_params(("arbitrary",)),
        name="moe_combine",
    )(slot_of, slot_of, ys_packed, topw)


def _expert_prep_kernel(w_ref, o_ref):
    n_in = 2 * F_EXPERT
    n_out = 2 * F_PAD
    src = lax.broadcasted_iota(jnp.int32, (n_in, n_out), 0)
    dst = lax.broadcasted_iota(jnp.int32, (n_in, n_out), 1)
    perm = jnp.where(dst == (src & 1) * F_PAD + (src >> 1), 1.0, 0.0).astype(BF16)
    o_ref[0] = jnp.dot(w_ref[0].astype(BF16), perm, preferred_element_type=F32).astype(o_ref.dtype)


def expert_gate_up_prep(w_gu):
    e, d, n_in = w_gu.shape
    tk = 1024
    return pl.pallas_call(
        _expert_prep_kernel,
        grid=(e, d // tk),
        in_specs=[pl.BlockSpec((1, tk, n_in), lambda s, k: (s, k, 0))],
        out_specs=pl.BlockSpec((1, tk, 2 * F_PAD), lambda s, k: (s, k, 0)),
        out_shape=jax.ShapeDtypeStruct((e, d, 2 * F_PAD), BF16),
        compiler_params=_params(("parallel", "parallel")),
        name="expert_gate_up_prep",
    )(w_gu)


def _expert_small_weights(b_gu, w_dn, b_dn):
    e = b_gu.shape[0]
    pad = ((0, 0), (0, F_PAD - F_EXPERT))
    bgu = jnp.concatenate([jnp.pad(b_gu[:, 0::2], pad), jnp.pad(b_gu[:, 1::2], pad)], axis=1).reshape(e, 1, 2 * F_PAD)
    wdn = jnp.pad(w_dn, ((0, 0), (0, F_PAD - F_EXPERT), (0, 0))).astype(BF16)
    return bgu, wdn, b_dn.reshape(e, 1, -1)


def moe_ffn(x, mods, g, router_w, router_b, w_gu, b_gu, w_dn, b_dn, mod_of_tile):
    hp, topi, topw = ffn_prep(x, mods, g, router_w, router_b, mod_of_tile)
    tile_e, n_valid, src_token, slot_of = routing_tables(topi)
    wgu = expert_gate_up_prep(w_gu)
    bgu, wdn, bdn = _expert_small_weights(b_gu, w_dn, b_dn)
    ys = expert_sweep(hp, tile_e, n_valid, src_token, wgu, bgu, wdn, bdn)
    return moe_combine(ys, slot_of, topw)


def _final_kernel(x_ref, y_ref, mod_ref, g_ref, o_ref):
    x = x_ref[...] + mod_ref[0, 5:6, :] * y_ref[...].astype(F32)
    ms = jnp.mean(x * x, axis=-1, keepdims=True)
    o_ref[...] = x * lax.rsqrt(ms + NORM_EPS) * g_ref[...]


def final_norm(x, y, mods, g, n_rows, mod_of_tile):
    d = x.shape[1]
    tm = PREP_TILE
    tiles_per_row_tile = ROW_TILE // tm
    return pl.pallas_call(
        _final_kernel,
        grid=(n_rows // tm,),
        in_specs=[
            pl.BlockSpec((tm, d), lambda i: (i, 0)),
            pl.BlockSpec((tm, d), lambda i: (i, 0)),
            pl.BlockSpec((1, N_MOD, d), lambda i: (mod_of_tile(i // tiles_per_row_tile), 0, 0)),
            pl.BlockSpec((1, d), lambda i: (0, 0)),
        ],
        out_specs=pl.BlockSpec((tm, d), lambda i: (i, 0)),
        out_shape=jax.ShapeDtypeStruct((n_rows, d), F32),
        compiler_params=_params(("parallel",)),
        name="final_norm",
    )(x, y, mods, g.reshape(1, d))


def _rope_tables(seq, ctx_len):
    n_freq = HEAD_DIM // 4
    rows = seq // GRID_W
    row = jnp.repeat(jnp.arange(rows, dtype=F32), GRID_W)
    col = jnp.tile(jnp.arange(GRID_W, dtype=F32), rows)
    inv_freq = ROPE_THETA ** (-jnp.arange(n_freq, dtype=F32) / n_freq)
    ang_r = row[:, None] * inv_freq
    ang_c = col[:, None] * inv_freq
    cos = jnp.concatenate([jnp.cos(ang_r)] * 2 + [jnp.cos(ang_c)] * 2, axis=-1)
    sin = jnp.concatenate([-jnp.sin(ang_r), jnp.sin(ang_r), -jnp.sin(ang_c), jnp.sin(ang_c)], axis=-1)
    cos = jnp.concatenate([jnp.ones((ctx_len, HEAD_DIM), F32), cos], axis=0)
    sin = jnp.concatenate([jnp.zeros((ctx_len, HEAD_DIM), F32), sin], axis=0)
    return cos, sin


def _pad_lanes(v):
    v = v.reshape(1, -1).astype(F32)
    return jnp.pad(v, ((0, 0), (0, LANES - v.shape[1])))


def kernel(x, c, ctx, c_ctx, ada_w, ada_b, norm_mix_g, norm_ffn_g, router_w, router_b, exp_w_gu, exp_b_gu,
           exp_w_down, exp_b_down, ev_w_in, ev_w_out, ev_q_g, ev_k_g, ev_conv_w, ev_conv_b, ev_a_log,
           ev_dt_bias, ev_d_skip, ev_ssm_g, od_w_in, od_w_out, od_sinks, final_g):
    batch, seq, d = x.shape
    ctx_len = ctx.shape[1]
    n_lat = batch * seq
    m = n_lat + batch * ctx_len
    assert d == D_MODEL and seq % ROW_TILE == 0 and batch * ctx_len == ROW_TILE and ctx_len == ATTN_TQ
    assert batch + 1 <= 8

    xs = jnp.concatenate([x.reshape(n_lat, d), ctx.reshape(batch * ctx_len, d)], axis=0)
    cvecs = jnp.concatenate([c, c_ctx[None, :], jnp.zeros((8 - batch - 1, d), F32)], axis=0)
    mods_all = ada_mods(cvecs, ada_w, ada_b).reshape(ada_w.shape[0], 8, N_MOD, d)[:, :batch + 1]

    tiles_per_batch = seq // ROW_TILE

    def mod_of_tile(i):
        return jnp.minimum(i // tiles_per_batch, batch)

    prep_ctx0 = n_lat // PREP_TILE
    pos_tiles = seq // PREP_TILE

    def pos_of_tile(i):
        return jnp.where(i < prep_ctx0, 1 + i % pos_tiles, 0)

    cos_t, sin_t = _rope_tables(seq, ctx_len)
    scale = HEAD_DIM ** -0.5 * LOG2_E

    prev = None
    for i in range(DEPTH):
        p = i // 2
        mods = mods_all[i]
        if i % 2 == 0:
            w_in = ev_w_in[p]
            w_main = w_in[:, :EVEN_MAIN].astype(BF16)
            w_dt = jnp.pad(w_in[:, EVEN_MAIN:], ((0, 0), (0, LANES - 2 * B_HEADS))).astype(BF16)
            yproj, dt_raw = in_projection(xs, prev, mods, norm_mix_g[i], w_main, w_dt, mod_of_tile)
            gain_q = (jnp.tile(ev_q_g[p], A_HEADS) * scale).reshape(1, -1)
            gain_k = jnp.tile(ev_k_g[p], A_KV_HEADS).reshape(1, -1)
            q_t = head_prep(yproj, 0, A_Q, (gain_q, cos_t, sin_t), pos_of_tile, use_norm=True, transpose=True)
            k = head_prep(yproj, A_Q, A_KV, (gain_k, cos_t, sin_t), pos_of_tile, use_norm=True, transpose=False)
            v_t = head_prep(yproj, EVEN_QK, A_KV, None, pos_of_tile, use_norm=False, transpose=True)
            attn = attention(q_t, k, v_t, None, batch=batch, seq=seq, ctx_len=ctx_len, n_heads=A_HEADS,
                             n_kv=A_KV_HEADS, windowed=False)
            z_col = A_Q + 2 * A_KV
            conv_w = jnp.pad(ev_conv_w[p], ((0, 8 - B_CONV_W), (0, 0)))
            xbc = conv_silu(yproj, conv_w, ev_conv_b[p].reshape(1, -1), col0=z_col + B_D_INNER, seq=seq,
                            n_lat_rows=n_lat)
            y_f, y_b = ssd_scan(xbc, dt_raw, _pad_lanes(ev_dt_bias[p]), _pad_lanes(ev_a_log[p]),
                                batch=batch, seq=seq, ctx_len=ctx_len)
            skip_row = jnp.repeat(ev_d_skip[p, 0] + ev_d_skip[p, 1], B_HEAD_DIM).reshape(1, -1)
            ssm = gated_norm(y_f, y_b, xbc, yproj, skip_row, ev_ssm_g[p].reshape(1, -1), z_col=z_col)
            xs = out_projection([attn, ssm], ev_w_out[p].astype(BF16), xs, prev, mods, mod_of_tile)
        else:
            yproj, _ = in_projection(xs, prev, mods, norm_mix_g[i], od_w_in[p].astype(BF16), None, mod_of_tile)
            gain_q = jnp.full((1, C_Q), scale, F32)
            gain_k = jnp.ones((1, C_KV), F32)
            q_t = head_prep(yproj, 0, C_Q, (gain_q, cos_t, sin_t), pos_of_tile, use_norm=False, transpose=True)
            k = head_prep(yproj, C_Q, C_KV, (gain_k, cos_t, sin_t), pos_of_tile, use_norm=False, transpose=False)
            v_t = head_prep(yproj, ODD_QK, C_KV, None, pos_of_tile, use_norm=False, transpose=True)
            attn = attention(q_t, k, v_t, od_sinks[p], batch=batch, seq=seq, ctx_len=ctx_len, n_heads=C_HEADS,
                             n_kv=C_KV_HEADS, windowed=True)
            xs = out_projection([attn], od_w_out[p].astype(BF16), xs, prev, mods, mod_of_tile)
        rw = jnp.pad(router_w[i], ((0, 0), (0, LANES - N_EXPERTS)))
        y = moe_ffn(xs, mods, norm_ffn_g[i], rw, _pad_lanes(router_b[i]), exp_w_gu[i], exp_b_gu[i],
                    exp_w_down[i], exp_b_down[i], mod_of_tile)
        prev = (y, mods)

    out = final_norm(xs, prev[0], prev[1], final_g, n_lat, mod_of_tile)
    return out.reshape(batch, seq, d)
```

```python
import functools
import math

import jax
import jax.numpy as jnp
import numpy as np
from jax import lax
from jax.experimental import pallas as pl
from jax.experimental.pallas import tpu as pltpu

F32 = jnp.float32
BF16 = jnp.bfloat16

D_MODEL = 4096
HEAD_DIM = 128
GRID_W = 64
ROPE_THETA = 10000.0
NORM_EPS = 1e-6
N_MOD = 6
DEPTH = 4

A_HEADS = D_MODEL // (2 * HEAD_DIM)
A_KV_HEADS = A_HEADS // 4
A_Q = A_HEADS * HEAD_DIM
A_KV = A_KV_HEADS * HEAD_DIM

B_D_INNER = D_MODEL // 2
B_HEAD_DIM = 64
B_HEADS = B_D_INNER // B_HEAD_DIM
B_GROUPS = 4
B_HPG = B_HEADS // B_GROUPS
B_STATE = 128
B_GN = B_GROUPS * B_STATE
B_CONV_CH = B_D_INNER + 2 * B_GN
B_CONV_W = 5
B_CHUNK = 128
B_GROUP_CH = B_D_INNER // B_GROUPS

C_HEADS = D_MODEL // HEAD_DIM
C_KV_HEADS = C_HEADS // 4
C_Q = C_HEADS * HEAD_DIM
C_KV = C_KV_HEADS * HEAD_DIM
C_WINDOW = 128

N_EXPERTS = 32
TOP_K = 4
F_EXPERT = 192
SWIGLU_LIMIT = 7.0
SWIGLU_ALPHA = 1.702
F_PAD = 256
GROUP_TILE = 256
COMBINE_TILE = 128
SORT_KEY_SHIFT = 17
HALF_D = D_MODEL // 2

LANES = 128
ROW_TILE = 512
COL_TILE = 512
ATTN_TQ = 256
ATTN_TK = 512
ATTN_L_ROWS = 16
LOG2_E = math.log2(math.e)
PREP_TILE = 256
VMEM_LIMIT = 56 * 1024 * 1024

EVEN_MAIN = A_Q + 2 * A_KV + B_D_INNER + B_CONV_CH
EVEN_QK = A_Q + A_KV
ODD_IN = C_Q + 2 * C_KV
ODD_QK = C_Q + C_KV


def _params(sem):
    return pltpu.CompilerParams(dimension_semantics=sem, vmem_limit_bytes=VMEM_LIMIT)


def _sigmoid(v):
    return 1.0 / (1.0 + jnp.exp(-v))


def _softplus(v):
    return jnp.maximum(v, 0.0) + jnp.log(1.0 + jnp.exp(-jnp.abs(v)))


def _ada_kernel(c_ref, w_ref, b_ref, o_ref):
    c = c_ref[...]
    s = (c * _sigmoid(c)).astype(BF16)
    o_ref[0] = jnp.dot(s, w_ref[0].astype(BF16), preferred_element_type=F32) + b_ref[0]


def ada_mods(cvecs, ada_w, ada_b):
    depth, d, n = ada_w.shape
    tn = COL_TILE
    return pl.pallas_call(
        _ada_kernel,
        grid=(depth, n // tn),
        in_specs=[
            pl.BlockSpec((8, d), lambda l, j: (0, 0)),
            pl.BlockSpec((1, d, tn), lambda l, j: (l, 0, j)),
            pl.BlockSpec((1, 1, tn), lambda l, j: (l, 0, j)),
        ],
        out_specs=pl.BlockSpec((1, 8, tn), lambda l, j: (l, 0, j)),
        out_shape=jax.ShapeDtypeStruct((depth, 8, n), F32),
        compiler_params=_params(("parallel", "parallel")),
        name="ada_mods",
    )(cvecs, ada_w, ada_b.reshape(depth, 1, n))


def _norm_mod(x, g, shift, scale):
    ms = jnp.mean(x * x, axis=-1, keepdims=True)
    return (x * lax.rsqrt(ms + NORM_EPS) * g) * (1.0 + scale) + shift


def _inproj_kernel(*refs, has_prev, has_dt):
    it = iter(refs)
    x_ref = next(it)
    if has_prev:
        y_ref = next(it)
        pmod_ref = next(it)
    mod_ref = next(it)
    g_ref = next(it)
    w_ref = next(it)
    if has_dt:
        wdt_ref = next(it)
    o_ref = next(it)
    if has_dt:
        dt_ref = next(it)
    h_scr = next(it)

    @pl.when(pl.program_id(1) == 0)
    def _():
        x = x_ref[...]
        if has_prev:
            x = x + pmod_ref[0, 5:6, :] * y_ref[...].astype(F32)
        h = _norm_mod(x, g_ref[...], mod_ref[0, 0:1, :], mod_ref[0, 1:2, :])
        h_scr[...] = h.astype(BF16)
        if has_dt:
            dt_ref[...] = jnp.dot(h_scr[...], wdt_ref[...], preferred_element_type=F32)

    o_ref[...] = jnp.dot(h_scr[...], w_ref[...], preferred_element_type=F32).astype(o_ref.dtype)


def in_projection(x, prev, mods, g, w, w_dt, mod_of_tile):
    m, d = x.shape
    tm, tn = ROW_TILE, COL_TILE
    n = w.shape[0] * tn
    has_prev = prev is not None
    has_dt = w_dt is not None
    mod_spec = pl.BlockSpec((1, N_MOD, d), lambda i, j: (mod_of_tile(i), 0, 0))
    args = [x]
    specs = [pl.BlockSpec((tm, d), lambda i, j: (i, 0))]
    if has_prev:
        args += [prev[0], prev[1]]
        specs += [pl.BlockSpec((tm, d), lambda i, j: (i, 0)), mod_spec]
    args += [mods, g.reshape(1, d), w]
    specs += [mod_spec, pl.BlockSpec((1, d), lambda i, j: (0, 0)),
              pl.BlockSpec((None, d, tn), lambda i, j: (j, 0, 0))]
    out_shape = [jax.ShapeDtypeStruct((m, n), BF16)]
    out_specs = [pl.BlockSpec((tm, tn), lambda i, j: (i, j))]
    if has_dt:
        args.append(w_dt)
        specs.append(pl.BlockSpec((d, LANES), lambda i, j: (0, 0)))
        out_shape.append(jax.ShapeDtypeStruct((m, LANES), F32))
        out_specs.append(pl.BlockSpec((tm, LANES), lambda i, j: (i, 0)))
    res = pl.pallas_call(
        functools.partial(_inproj_kernel, has_prev=has_prev, has_dt=has_dt),
        grid=(m // tm, n // tn),
        in_specs=specs,
        out_specs=out_specs,
        out_shape=out_shape,
        scratch_shapes=[pltpu.VMEM((tm, d), BF16)],
        compiler_params=_params(("parallel", "arbitrary")),
        name="in_projection",
    )(*args)
    return res if has_dt else (res[0], None)


def _head_prep_kernel(*refs, use_norm, use_rope, transpose):
    y_ref = refs[0]
    o_ref = refs[-1]
    if use_rope:
        gain_ref, cos_ref, sin_ref = refs[1:4]
        cos = cos_ref[...]
        sin = sin_ref[...]
        lane = lax.broadcasted_iota(jnp.int32, cos.shape, 1)
        first_half = (lane & (HEAD_DIM // 4)) == 0
    for hh in range(COL_TILE // HEAD_DIM):
        sl = slice(hh * HEAD_DIM, (hh + 1) * HEAD_DIM)
        v = y_ref[:, sl].astype(F32)
        if use_norm:
            v = v * lax.rsqrt(jnp.mean(v * v, axis=-1, keepdims=True) + NORM_EPS)
        if use_rope:
            v = v * gain_ref[:, sl]
            swapped = jnp.where(first_half, pltpu.roll(v, HEAD_DIM - HEAD_DIM // 4, 1),
                                pltpu.roll(v, HEAD_DIM // 4, 1))
            v = v * cos + swapped * sin
        if transpose:
            o_ref[sl, :] = v.T.astype(o_ref.dtype)
        else:
            o_ref[:, sl] = v.astype(o_ref.dtype)


def head_prep(yproj, col0, n_cols, rope, pos_of_tile, *, use_norm, transpose):
    m = yproj.shape[0]
    tm = PREP_TILE
    c0 = col0 // COL_TILE
    use_rope = rope is not None
    args = [yproj]
    specs = [pl.BlockSpec((tm, COL_TILE), lambda i, j: (i, c0 + j))]
    if use_rope:
        args += list(rope)
        specs += [pl.BlockSpec((1, COL_TILE), lambda i, j: (0, j)),
                  pl.BlockSpec((tm, HEAD_DIM), lambda i, j: (pos_of_tile(i), 0)),
                  pl.BlockSpec((tm, HEAD_DIM), lambda i, j: (pos_of_tile(i), 0))]
    if transpose:
        out_spec = pl.BlockSpec((COL_TILE, tm), lambda i, j: (j, i))
        out_shape = jax.ShapeDtypeStruct((n_cols, m), BF16)
    else:
        out_spec = pl.BlockSpec((tm, COL_TILE), lambda i, j: (i, j))
        out_shape = jax.ShapeDtypeStruct((m, n_cols), BF16)
    return pl.pallas_call(
        functools.partial(_head_prep_kernel, use_norm=use_norm, use_rope=use_rope, transpose=transpose),
        grid=(m // tm, n_cols // COL_TILE),
        in_specs=specs,
        out_specs=out_spec,
        out_shape=out_shape,
        compiler_params=_params(("parallel", "parallel")),
        name="head_prep",
    )(*args)


def _attn_kernel(*refs, group, n_lat_tiles, seq, windowed, has_sink):
    it = iter(refs)
    if has_sink:
        sink_ref = next(it)
    qt_ref, kc_ref, vtc_ref, kl_ref, vtl_ref, o_ref, m_scr, acc_scr, s_scr = (next(it) for _ in range(9))
    tq = ATTN_TQ
    kvh = pl.program_id(1)
    qi = pl.program_id(2)
    cols = group * tq

    q_t = jnp.concatenate([qt_ref[g * HEAD_DIM:(g + 1) * HEAD_DIM, :] for g in range(group)], axis=1)

    if has_sink:
        m_scr[...] = jnp.concatenate(
            [jnp.full((1, tq), sink_ref[kvh * group + g] * LOG2_E, F32) for g in range(group)], axis=1)
        acc_scr[...] = jnp.concatenate([jnp.zeros((HEAD_DIM, cols), F32), jnp.ones((ATTN_L_ROWS, cols), F32)], axis=0)
    else:
        m_scr[...] = jnp.full((1, cols), -jnp.inf, F32)
        acc_scr[...] = jnp.zeros((HEAD_DIM + ATTN_L_ROWS, cols), F32)

    def scores(k):
        return jnp.dot(k, q_t, preferred_element_type=F32)

    def softmax_pv(s, v_t):
        m_prev = m_scr[...]
        m_new = jnp.maximum(m_prev, jnp.max(s, axis=0, keepdims=True))
        alpha = jnp.exp2(m_prev - m_new)
        p = jnp.exp2(s - m_new).astype(BF16)
        v_ext = jnp.concatenate([v_t, jnp.ones((ATTN_L_ROWS, v_t.shape[1]), BF16)], axis=0)
        acc_scr[...] = alpha * acc_scr[...] + jnp.dot(v_ext, p, preferred_element_type=F32)
        m_scr[...] = m_new

    @pl.when(qi >= n_lat_tiles)
    def _():
        softmax_pv(scores(kc_ref[...]), vtc_ref[...])

    @pl.when(qi < n_lat_tiles)
    def _():
        s_ctx = scores(kc_ref[...])
        if windowed:
            band = tq + 2 * C_WINDOW
            start = jnp.clip(qi * tq - C_WINDOW, 0, seq - band)
            start = pl.multiple_of(start, C_WINDOW)
            k_pos = start + lax.broadcasted_iota(jnp.int32, (band, cols), 0)
            q_pos = qi * tq + (lax.broadcasted_iota(jnp.int32, (band, cols), 1) & (tq - 1))
            mask = jnp.abs(k_pos - q_pos) <= C_WINDOW
            s_scr[0] = jnp.where(mask, scores(kl_ref[pl.ds(start, band), :]), -jnp.inf)
            softmax_pv(s_ctx, vtc_ref[...])
            softmax_pv(s_scr[0], vtl_ref[:, pl.ds(start, band)])
        else:
            n_chunks = seq // ATTN_TK

            def k_chunk(c):
                return kl_ref[pl.ds(pl.multiple_of(c * ATTN_TK, ATTN_TK), ATTN_TK), :]

            def v_chunk(c):
                return vtl_ref[:, pl.ds(pl.multiple_of(c * ATTN_TK, ATTN_TK), ATTN_TK)]

            s_scr[0] = scores(k_chunk(0))
            softmax_pv(s_ctx, vtc_ref[...])

            def body(c2, carry):
                c = 2 * c2
                s_scr[1] = scores(k_chunk(c + 1))
                softmax_pv(s_scr[0], v_chunk(c))
                s_scr[0] = scores(k_chunk(jnp.minimum(c + 2, n_chunks - 1)))
                softmax_pv(s_scr[1], v_chunk(c + 1))
                return carry
            if n_chunks >= 2:
                lax.fori_loop(0, n_chunks // 2, body, 0)
            if n_chunks % 2:
                softmax_pv(s_scr[0], v_chunk(n_chunks - 1))

    out_t = acc_scr[0:HEAD_DIM, :] / acc_scr[HEAD_DIM:HEAD_DIM + 1, :]
    for g in range(group):
        o_ref[:, g * HEAD_DIM:(g + 1) * HEAD_DIM] = out_t[:, g * tq:(g + 1) * tq].T.astype(o_ref.dtype)


def attention(q_t, k, v_t, sinks, *, batch, seq, ctx_len, n_heads, n_kv, windowed):
    m = k.shape[0]
    group = n_heads // n_kv
    tq = ATTN_TQ
    assert ctx_len == tq and seq % ATTN_TK == 0 and tq + 2 * C_WINDOW == ATTN_TK and tq & (tq - 1) == 0
    n_lat_tiles = seq // tq
    ctx_block0 = batch * seq // ctx_len
    has_sink = sinks is not None

    def q_tile(b, i):
        return jnp.where(i < n_lat_tiles, b * n_lat_tiles + i, batch * n_lat_tiles + b)

    in_specs = [
        pl.BlockSpec((group * HEAD_DIM, tq), lambda b, h, i, *_: (h, q_tile(b, i))),
        pl.BlockSpec((ctx_len, HEAD_DIM), lambda b, h, i, *_: (ctx_block0 + b, h)),
        pl.BlockSpec((HEAD_DIM, ctx_len), lambda b, h, i, *_: (h, ctx_block0 + b)),
        pl.BlockSpec((seq, HEAD_DIM), lambda b, h, i, *_: (b, h)),
        pl.BlockSpec((HEAD_DIM, seq), lambda b, h, i, *_: (h, b)),
    ]
    args = [q_t, k, v_t, k, v_t]
    cols = group * tq
    grid_spec = pltpu.PrefetchScalarGridSpec(
        num_scalar_prefetch=1 if has_sink else 0,
        grid=(batch, n_kv, n_lat_tiles + 1),
        in_specs=in_specs,
        out_specs=pl.BlockSpec((tq, group * HEAD_DIM), lambda b, h, i, *_: (q_tile(b, i), h)),
        scratch_shapes=[pltpu.VMEM((1, cols), F32), pltpu.VMEM((HEAD_DIM + ATTN_L_ROWS, cols), F32),
                        pltpu.VMEM((2, ATTN_TK, cols), F32)],
    )
    if has_sink:
        args = [sinks] + args
    return pl.pallas_call(
        functools.partial(_attn_kernel, group=group, n_lat_tiles=n_lat_tiles, seq=seq,
                          windowed=windowed, has_sink=has_sink),
        grid_spec=grid_spec,
        out_shape=jax.ShapeDtypeStruct((m, n_heads * HEAD_DIM), BF16),
        compiler_params=_params(("parallel", "parallel", "arbitrary")),
        name="attention_window" if windowed else "attention_global",
    )(*args)


CONV_HALO = 16


def _conv_kernel(prev_ref, cur_ref, next_ref, w_ref, b_ref, o_ref, *, tiles_per_seq, n_lat_tiles):
    i = pl.program_id(0)
    tm = cur_ref.shape[0]
    in_lat = i < n_lat_tiles
    pos = i % tiles_per_seq
    has_prev = jnp.logical_and(in_lat, pos != 0)
    has_next = jnp.logical_and(in_lat, pos != tiles_per_seq - 1)
    prev = jnp.where(has_prev, prev_ref[...].astype(F32), 0.0)
    nxt = jnp.where(has_next, next_ref[...].astype(F32), 0.0)
    ext = jnp.concatenate([prev, cur_ref[...].astype(F32), nxt], axis=0)
    n_ext = tm + 2 * CONV_HALO
    acc = jnp.zeros((tm, ext.shape[1]), F32) + b_ref[...]
    for k in range(B_CONV_W):
        shift = (B_CONV_W // 2 - k) % n_ext
        shifted = ext if shift == 0 else pltpu.roll(ext, shift, 0)
        acc = acc + shifted[CONV_HALO:CONV_HALO + tm, :] * w_ref[k:k + 1, :]
    o_ref[...] = (acc * _sigmoid(acc)).astype(o_ref.dtype)


def conv_silu(yproj, conv_w, conv_b, *, col0, seq, n_lat_rows):
    m = yproj.shape[0]
    tm = PREP_TILE
    tiles_per_seq = seq // tm
    n_lat_tiles = n_lat_rows // tm
    c0 = col0 // COL_TILE
    halo_per_tile = tm // CONV_HALO
    n_halo_blocks = m // CONV_HALO
    return pl.pallas_call(
        functools.partial(_conv_kernel, tiles_per_seq=tiles_per_seq, n_lat_tiles=n_lat_tiles),
        grid=(m // tm, B_CONV_CH // COL_TILE),
        in_specs=[
            pl.BlockSpec((CONV_HALO, COL_TILE), lambda i, j: (jnp.maximum(i * halo_per_tile - 1, 0), c0 + j)),
            pl.BlockSpec((tm, COL_TILE), lambda i, j: (i, c0 + j)),
            pl.BlockSpec((CONV_HALO, COL_TILE),
                         lambda i, j: (jnp.minimum((i + 1) * halo_per_tile, n_halo_blocks - 1), c0 + j)),
            pl.BlockSpec((8, COL_TILE), lambda i, j: (0, j)),
            pl.BlockSpec((1, COL_TILE), lambda i, j: (0, j)),
        ],
        out_specs=pl.BlockSpec((tm, COL_TILE), lambda i, j: (i, j)),
        out_shape=jax.ShapeDtypeStruct((m, B_CONV_CH), BF16),
        compiler_params=_params(("parallel", "parallel")),
        name="conv_silu",
    )(yproj, yproj, yproj, conv_w, conv_b)


def _ssd_kernel(xf_ref, bf_ref, cf_ref, dtf_ref, xb_ref, bb_ref, cb_ref, dtb_ref, bias_ref, alog_ref,
                yf_ref, yb_ref, h_scr):
    t = pl.program_id(1)
    q = B_CHUNK

    @pl.when(t == 0)
    def _():
        h_scr[...] = jnp.zeros(h_scr.shape, F32)

    row = lax.broadcasted_iota(jnp.int32, (q, q), 0)
    col = lax.broadcasted_iota(jnp.int32, (q, q), 1)
    lane_lo = lax.broadcasted_iota(jnp.int32, (q, LANES), 1) < B_HEAD_DIM
    lane_lo_row = lane_lo[0:1, :]
    a_coef = -jnp.exp(alog_ref[...])

    dirs = ((xf_ref, bf_ref, cf_ref, dtf_ref, yf_ref, col <= row, q - 1),
            (xb_ref, bb_ref, cb_ref, dtb_ref, yb_ref, col >= row, 0))
    for d, (x_ref, b_ref, c_ref, dt_ref, y_ref, tri, last_row) in enumerate(dirs):
        dt = _softplus(dt_ref[...] + bias_ref[...])
        ld = dt * a_coef
        cum = jnp.dot(tri.astype(F32), ld, preferred_element_type=F32, precision=lax.Precision.HIGHEST)
        cum_t = cum.T
        last = cum[last_row:last_row + 1, :]
        e_cum = jnp.exp(cum)
        e_rest = jnp.exp(last - cum)
        e_last = jnp.exp(last)

        def pick(arr, l0, lo=lane_lo):
            return jnp.where(lo, arr[:, l0:l0 + 1], arr[:, l0 + 1:l0 + 2])

        for g in range(B_GROUPS):
            bm = b_ref[:, g * B_STATE:(g + 1) * B_STATE]
            cm = c_ref[:, g * B_STATE:(g + 1) * B_STATE]
            cbm = lax.dot_general(cm, bm, (((1,), (1,)), ((), ())), preferred_element_type=F32)
            bm_t = bm.astype(F32).T.astype(BF16)
            h_prev = h_scr[d, g]
            y_off = jnp.dot(cm, h_prev.astype(BF16), preferred_element_type=F32)
            xw_parts = []
            dec_parts = []
            for p in range(B_HPG // 2):
                l0 = d * (B_GROUPS * B_HPG) + g * B_HPG + 2 * p
                c0 = g * B_GROUP_CH + p * LANES
                xdt = x_ref[:, c0:c0 + LANES].astype(F32) * pick(dt, l0)
                y_p = y_off[:, p * LANES:(p + 1) * LANES] * pick(e_cum, l0)
                for half in range(2):
                    l = l0 + half
                    seg = jnp.exp(jnp.where(tri, cum[:, l:l + 1] - cum_t[l:l + 1, :], -jnp.inf))
                    keep = lane_lo if half == 0 else jnp.logical_not(lane_lo)
                    y_p = y_p + jnp.dot((cbm * seg).astype(BF16), jnp.where(keep, xdt, 0.0).astype(BF16),
                                        preferred_element_type=F32)
                y_ref[:, c0:c0 + LANES] = y_p.astype(y_ref.dtype)
                xw_parts.append((xdt * pick(e_rest, l0)).astype(BF16))
                dec_parts.append(pick(e_last, l0, lane_lo_row))
            xw = jnp.concatenate(xw_parts, axis=1)
            dec = jnp.concatenate(dec_parts, axis=1)
            h_scr[d, g] = h_prev * dec + jnp.dot(bm_t, xw, preferred_element_type=F32)


def ssd_scan(xbc, dt_raw, dt_bias_row, a_log_row, *, batch, seq, ctx_len):
    m = xbc.shape[0]
    q = B_CHUNK
    nc_lat, nc_ctx = seq // q, ctx_len // q
    n_steps = nc_ctx + nc_lat
    ctx0 = batch * nc_lat

    def fwd_blk(b, t):
        return jnp.where(t < nc_ctx, ctx0 + b * nc_ctx + t, b * nc_lat + t - nc_ctx)

    def bwd_blk(b, t):
        return jnp.where(t < nc_ctx, ctx0 + b * nc_ctx + (nc_ctx - 1 - t), b * nc_lat + (n_steps - 1 - t))

    xb, bb, cb = 0, B_D_INNER // B_GN, B_D_INNER // B_GN + 1

    def specs(blk):
        return [
            pl.BlockSpec((q, B_D_INNER), lambda b, t: (blk(b, t), xb)),
            pl.BlockSpec((q, B_GN), lambda b, t: (blk(b, t), bb)),
            pl.BlockSpec((q, B_GN), lambda b, t: (blk(b, t), cb)),
            pl.BlockSpec((q, LANES), lambda b, t: (blk(b, t), 0)),
        ]

    row_spec = pl.BlockSpec((1, LANES), lambda b, t: (0, 0))
    return pl.pallas_call(
        _ssd_kernel,
        grid=(batch, n_steps),
        in_specs=specs(fwd_blk) + specs(bwd_blk) + [row_spec, row_spec],
        out_specs=[pl.BlockSpec((q, B_D_INNER), lambda b, t: (fwd_blk(b, t), 0)),
                   pl.BlockSpec((q, B_D_INNER), lambda b, t: (bwd_blk(b, t), 0))],
        out_shape=[jax.ShapeDtypeStruct((m, B_D_INNER), BF16)] * 2,
        scratch_shapes=[pltpu.VMEM((2, B_GROUPS, B_STATE, B_GROUP_CH), F32)],
        compiler_params=_params(("parallel", "arbitrary")),
        name="ssd_scan",
    )(xbc, xbc, xbc, dt_raw, xbc, xbc, xbc, dt_raw, dt_bias_row, a_log_row)


def _gated_norm_kernel(yf_ref, yb_ref, x_ref, z_ref, skip_ref, g_ref, o_ref):
    z = z_ref[...].astype(F32)
    y = yf_ref[...].astype(F32) + yb_ref[...].astype(F32) + skip_ref[...] * x_ref[...].astype(F32)
    yz = y * (z * _sigmoid(z))
    ms = jnp.mean(yz * yz, axis=-1, keepdims=True)
    o_ref[...] = (yz * lax.rsqrt(ms + NORM_EPS) * g_ref[...]).astype(o_ref.dtype)


def gated_norm(y_f, y_b, xbc, yproj, skip_row, g_row, *, z_col):
    m = y_f.shape[0]
    tm = PREP_TILE
    w = B_GROUP_CH
    z0 = z_col // w
    blk = pl.BlockSpec((tm, w), lambda i, j: (i, j))
    row = pl.BlockSpec((1, w), lambda i, j: (0, j))
    return pl.pallas_call(
        _gated_norm_kernel,
        grid=(m // tm, B_GROUPS),
        in_specs=[blk, blk, blk, pl.BlockSpec((tm, w), lambda i, j: (i, z0 + j)), row, row],
        out_specs=blk,
        out_shape=jax.ShapeDtypeStruct((m, B_D_INNER), BF16),
        compiler_params=_params(("parallel", "parallel")),
        name="gated_norm",
    )(y_f, y_b, xbc, yproj, skip_row, g_row)


def _outproj_kernel(*refs, n_lhs, has_prev):
    it = iter(refs)
    lhs = [next(it) for _ in range(n_lhs)]
    ws = [next(it) for _ in range(n_lhs)]
    x_ref = next(it)
    if has_prev:
        y_ref = next(it)
        pmod_ref = next(it)
    mod_ref = next(it)
    o_ref = next(it)
    acc = jnp.dot(lhs[0][...], ws[0][...], preferred_element_type=F32)
    for a, w in zip(lhs[1:], ws[1:]):
        acc = acc + jnp.dot(a[...], w[...], preferred_element_type=F32)
    x = x_ref[...]
    if has_prev:
        x = x + pmod_ref[0, 5:6, :] * y_ref[...].astype(F32)
    o_ref[...] = x + mod_ref[0, 2:3, :] * acc


def out_projection(lhs, w, x, prev, mods, mod_of_tile):
    m, d = x.shape
    tm, tn = ROW_TILE, COL_TILE
    kw = lhs[0].shape[1]
    has_prev = prev is not None
    mod_spec = pl.BlockSpec((1, N_MOD, tn), lambda i, j: (mod_of_tile(i), 0, j))
    specs = [pl.BlockSpec((tm, kw), lambda i, j: (i, 0)) for _ in lhs]
    specs += [pl.BlockSpec((None, kw, tn), functools.partial(lambda i, j, r: (j, r, 0), r=r))
              for r in range(len(lhs))]
    args = list(lhs) + [w] * len(lhs) + [x]
    specs.append(pl.BlockSpec((tm, tn), lambda i, j: (i, j)))
    if has_prev:
        args += [prev[0], prev[1]]
        specs += [pl.BlockSpec((tm, tn), lambda i, j: (i, j)), mod_spec]
    args.append(mods)
    specs.append(mod_spec)
    return pl.pallas_call(
        functools.partial(_outproj_kernel, n_lhs=len(lhs), has_prev=has_prev),
        grid=(m // tm, d // tn),
        in_specs=specs,
        out_specs=pl.BlockSpec((tm, tn), lambda i, j: (i, j)),
        out_shape=jax.ShapeDtypeStruct((m, d), F32),
        compiler_params=_params(("parallel", "parallel")),
        name="out_projection",
    )(*args)


def _gather_params(sem):
    return pltpu.CompilerParams(dimension_semantics=sem, vmem_limit_bytes=VMEM_LIMIT, disable_bounds_checks=True)


def _pack_bf16_pairs(v):
    bits = lax.bitcast_convert_type(v.astype(BF16).astype(F32), jnp.uint32)
    return bits[:, HALF_D:] | (bits[:, :HALF_D] >> 16)


def _unpack_bf16_pairs(w):
    lo = lax.bitcast_convert_type(w << 16, F32)
    hi = lax.bitcast_convert_type(w & jnp.uint32(0xFFFF0000), F32)
    return lo, hi


def _ffn_prep_kernel(x_ref, mod_ref, g_ref, rw_ref, rb_ref, hp_ref, topi_ref, topw_ref):
    h = _norm_mod(x_ref[...], g_ref[...], mod_ref[0, 3:4, :], mod_ref[0, 4:5, :])
    hp_ref[...] = _pack_bf16_pairs(h)
    logits = jnp.dot(h, rw_ref[...], preferred_element_type=F32, precision=lax.Precision.HIGHEST) + rb_ref[...]
    lane = lax.broadcasted_iota(jnp.int32, logits.shape, 1)
    work = jnp.where(lane < N_EXPERTS, logits, -jnp.inf)
    top = None
    denom = jnp.zeros((logits.shape[0], 1), F32)
    topi = jnp.zeros(logits.shape, jnp.int32)
    topw = jnp.zeros(logits.shape, F32)
    for k in range(TOP_K):
        mx = jnp.max(work, axis=-1, keepdims=True)
        idx = jnp.min(jnp.where(work == mx, lane, LANES), axis=-1, keepdims=True)
        if top is None:
            top = mx
        e = jnp.exp(mx - top)
        denom = denom + e
        topi = jnp.where(lane == k, idx, topi)
        topw = jnp.where(lane == k, e, topw)
        work = jnp.where(lane == idx, -jnp.inf, work)
    topi_ref[...] = topi
    topw_ref[...] = topw / denom


def ffn_prep(x, mods, g, router_w, router_b, mod_of_tile):
    m, d = x.shape
    tm = PREP_TILE
    tiles_per_row_tile = ROW_TILE // tm
    return pl.pallas_call(
        _ffn_prep_kernel,
        grid=(m // tm,),
        in_specs=[
            pl.BlockSpec((tm, d), lambda i: (i, 0)),
            pl.BlockSpec((1, N_MOD, d), lambda i: (mod_of_tile(i // tiles_per_row_tile), 0, 0)),
            pl.BlockSpec((1, d), lambda i: (0, 0)),
            pl.BlockSpec((d, LANES), lambda i: (0, 0)),
            pl.BlockSpec((1, LANES), lambda i: (0, 0)),
        ],
        out_specs=[pl.BlockSpec((tm, d // 2), lambda i: (i, 0)),
                   pl.BlockSpec((tm, LANES), lambda i: (i, 0)),
                   pl.BlockSpec((tm, LANES), lambda i: (i, 0))],
        out_shape=[jax.ShapeDtypeStruct((m, d // 2), jnp.uint32),
                   jax.ShapeDtypeStruct((m, LANES), jnp.int32),
                   jax.ShapeDtypeStruct((m, LANES), F32)],
        compiler_params=_params(("parallel",)),
        name="ffn_prep",
    )(x, mods, g.reshape(1, d), router_w, router_b)


def routing_tables(topi):
    m = topi.shape[0]
    ids = topi[:, :TOP_K].reshape(-1)
    n_assign = ids.shape[0]
    tg = GROUP_TILE
    assert n_assign < (1 << SORT_KEY_SHIFT) and n_assign % tg == 0 and m % COMBINE_TILE == 0
    n_tiles = (n_assign + N_EXPERTS * tg) // tg
    onehot = (ids[:, None] == jnp.arange(N_EXPERTS, dtype=jnp.int32)[None, :]).astype(jnp.int32)
    csum = jnp.cumsum(onehot, axis=0)
    counts = csum[-1]
    rank = jnp.take_along_axis(csum, ids[:, None], axis=1)[:, 0] - 1
    padded = ((counts + tg - 1) // tg) * tg
    g_end = jnp.cumsum(padded)
    g_start = g_end - padded
    u_start = jnp.cumsum(counts) - counts
    slot = g_start[ids] + rank
    keys = jnp.sort(ids * (1 << SORT_KEY_SHIFT) + jnp.arange(n_assign, dtype=jnp.int32))
    order = keys & ((1 << SORT_KEY_SHIFT) - 1)
    tile_start = jnp.arange(n_tiles, dtype=jnp.int32) * tg
    tile_e = jnp.sum((tile_start[:, None] >= g_end[None, :]).astype(jnp.int32), axis=1)
    n_valid = (g_end[-1] // tg).reshape(1)
    tile_ec = jnp.minimum(tile_e, N_EXPERTS - 1)
    s = jnp.arange(n_tiles * tg, dtype=jnp.int32)
    e_s = jnp.repeat(tile_ec, tg)
    off = s - g_start[e_s]
    valid = jnp.logical_and(off < counts[e_s], jnp.repeat(tile_e, tg) < N_EXPERTS)
    r = jnp.clip(u_start[e_s] + off, 0, n_assign - 1)
    src_token = jnp.where(valid, order[r] // TOP_K, 0).reshape(n_tiles, 1, tg)
    tc = COMBINE_TILE
    slot_of = slot.reshape(m // tc, tc, TOP_K).transpose(0, 2, 1).reshape(m // tc, 1, TOP_K * tc)
    return tile_ec, n_valid, src_token, slot_of


def _row_copy(src_hbm, src_row, dst, dst_row, sem):
    return pltpu.make_async_copy(src_hbm.at[pl.ds(src_row, 1)], dst.at[pl.ds(dst_row, 1)], sem)


def _start_row_gather(idx_ref, n_rows, src_hbm, dst, sem):
    def body(r, carry):
        _row_copy(src_hbm, idx_ref[0, 0, r], dst, r, sem).start()
        return carry
    lax.fori_loop(0, n_rows, body, 0, unroll=8)


def _start_row_gather_inline(idx_ref, n_rows, src_hbm, dst, sem):
    for r in range(n_rows):
        _row_copy(src_hbm, idx_ref[0, 0, r], dst, r, sem).start()


def _wait_row_gather(n_rows, src_hbm, dst, sem):
    def body(r, carry):
        _row_copy(src_hbm, 0, dst, r, sem).wait()
        return carry
    lax.fori_loop(0, n_rows, body, 0, unroll=8)


def _expert_kernel(tile_e_ref, n_valid_ref, idx_ref, idx_next_ref, h_hbm, wgu_ref, bgu_ref, wdn_ref, bdn_ref,
                   o_ref, hbuf, sem):
    t = pl.program_id(0)
    n_valid = n_valid_ref[0]
    cur = t % 2

    @pl.when(t == 0)
    def _():
        _start_row_gather(idx_ref, GROUP_TILE, h_hbm, hbuf.at[0], sem.at[0])

    @pl.when(t < n_valid)
    def _():
        _wait_row_gather(GROUP_TILE, h_hbm, hbuf.at[cur], sem.at[cur])
        _start_row_gather_inline(idx_next_ref, GROUP_TILE, h_hbm, hbuf.at[1 - cur], sem.at[1 - cur])
        lo, hi = _unpack_bf16_pairs(hbuf[cur])
        h = jnp.concatenate([lo.astype(BF16), hi.astype(BF16)], axis=1)
        gu = jnp.dot(h, wgu_ref[0], preferred_element_type=F32) + bgu_ref[0]
        glu = jnp.minimum(gu[:, :F_PAD], SWIGLU_LIMIT)
        lin = jnp.clip(gu[:, F_PAD:], -SWIGLU_LIMIT, SWIGLU_LIMIT)
        act = glu * _sigmoid(SWIGLU_ALPHA * glu) * (lin + 1.0)
        y = jnp.dot(act.astype(BF16), wdn_ref[0], preferred_element_type=F32) + bdn_ref[0]
        o_ref[...] = _pack_bf16_pairs(y)

    @pl.when(t == n_valid)
    def _():
        _wait_row_gather(GROUP_TILE, h_hbm, hbuf.at[cur], sem.at[cur])

    @pl.when(t >= n_valid)
    def _():
        o_ref[...] = jnp.zeros(o_ref.shape, o_ref.dtype)


def expert_sweep(h_packed, tile_e, n_valid, src_token, wgu, bgu, wdn, bdn):
    n_tiles = src_token.shape[0]
    tg = GROUP_TILE
    d = D_MODEL
    grid_spec = pltpu.PrefetchScalarGridSpec(
        num_scalar_prefetch=2,
        grid=(n_tiles,),
        in_specs=[
            pl.BlockSpec((1, 1, tg), lambda t, te, nv: (t, 0, 0), memory_space=pltpu.SMEM),
            pl.BlockSpec((1, 1, tg), lambda t, te, nv: (jnp.minimum(t + 1, n_tiles - 1), 0, 0),
                         memory_space=pltpu.SMEM),
            pl.BlockSpec(memory_space=pl.ANY),
            pl.BlockSpec((1, d, 2 * F_PAD), lambda t, te, nv: (te[t], 0, 0)),
            pl.BlockSpec((1, 1, 2 * F_PAD), lambda t, te, nv: (te[t], 0, 0)),
            pl.BlockSpec((1, F_PAD, d), lambda t, te, nv: (te[t], 0, 0)),
            pl.BlockSpec((1, 1, d), lambda t, te, nv: (te[t], 0, 0)),
        ],
        out_specs=pl.BlockSpec((tg, d // 2), lambda t, te, nv: (t, 0)),
        scratch_shapes=[pltpu.VMEM((2, tg, d // 2), jnp.uint32), pltpu.SemaphoreType.DMA((2,))],
    )
    return pl.pallas_call(
        _expert_kernel,
        grid_spec=grid_spec,
        out_shape=jax.ShapeDtypeStruct((n_tiles * tg, d // 2), jnp.uint32),
        compiler_params=_gather_params(("arbitrary",)),
        name="expert_sweep",
    )(tile_e, n_valid, src_token, src_token, h_packed, wgu, bgu, wdn, bdn)


def _combine_kernel(slot_ref, slot_next_ref, ys_hbm, topw_ref, o_ref, buf, sem):
    i = pl.program_id(0)
    n = pl.num_programs(0)
    cur = i % 2
    n_rows = TOP_K * COMBINE_TILE

    @pl.when(i == 0)
    def _():
        _start_row_gather(slot_ref, n_rows, ys_hbm, buf.at[0], sem.at[0])

    _wait_row_gather(n_rows, ys_hbm, buf.at[cur], sem.at[cur])
    _start_row_gather_inline(slot_next_ref, n_rows, ys_hbm, buf.at[1 - cur], sem.at[1 - cur])
    w = topw_ref[...]
    acc_lo = jnp.zeros((COMBINE_TILE, HALF_D), F32)
    acc_hi = jnp.zeros((COMBINE_TILE, HALF_D), F32)
    for k in range(TOP_K):
        lo, hi = _unpack_bf16_pairs(buf[cur, k * COMBINE_TILE:(k + 1) * COMBINE_TILE, :])
        acc_lo = acc_lo + w[:, k:k + 1] * lo
        acc_hi = acc_hi + w[:, k:k + 1] * hi
    o_ref[:, :HALF_D] = acc_lo.astype(o_ref.dtype)
    o_ref[:, HALF_D:] = acc_hi.astype(o_ref.dtype)

    @pl.when(i == n - 1)
    def _():
        _wait_row_gather(n_rows, ys_hbm, buf.at[1 - cur], sem.at[1 - cur])


def moe_combine(ys_packed, slot_of, topw):
    m = topw.shape[0]
    tc = COMBINE_TILE
    n = m // tc
    d = D_MODEL
    return pl.pallas_call(
        _combine_kernel,
        grid=(n,),
        in_specs=[
            pl.BlockSpec((1, 1, TOP_K * tc), lambda i: (i, 0, 0), memory_space=pltpu.SMEM),
            pl.BlockSpec((1, 1, TOP_K * tc), lambda i: (jnp.minimum(i + 1, n - 1), 0, 0), memory_space=pltpu.SMEM),
            pl.BlockSpec(memory_space=pl.ANY),
            pl.BlockSpec((tc, LANES), lambda i: (i, 0)),
        ],
        out_specs=pl.BlockSpec((tc, d), lambda i: (i, 0)),
        out_shape=jax.ShapeDtypeStruct((m, d), BF16),
        scratch_shapes=[pltpu.VMEM((2, TOP_K * tc, d // 2), jnp.uint32), pltpu.SemaphoreType.DMA((2,))],
        compiler_params=_gather_params(("arbitrary",)),
        name="moe_combine",
    )(slot_of, slot_of, ys_packed, topw)


def _expert_prep_kernel(w_ref, o_ref):
    n_in = 2 * F_EXPERT
    n_out = 2 * F_PAD
    src = lax.broadcasted_iota(jnp.int32, (n_in, n_out), 0)
    dst = lax.broadcasted_iota(jnp.int32, (n_in, n_out), 1)
    perm = jnp.where(dst == (src & 1) * F_PAD + (src >> 1), 1.0, 0.0).astype(BF16)
    o_ref[0] = jnp.dot(w_ref[0].astype(BF16), perm, preferred_element_type=F32).astype(o_ref.dtype)


def expert_gate_up_prep(w_gu):
    e, d, n_in = w_gu.shape
    tk = 1024
    return pl.pallas_call(
        _expert_prep_kernel,
        grid=(e, d // tk),
        in_specs=[pl.BlockSpec((1, tk, n_in), lambda s, k: (s, k, 0))],
        out_specs=pl.BlockSpec((1, tk, 2 * F_PAD), lambda s, k: (s, k, 0)),
        out_shape=jax.ShapeDtypeStruct((e, d, 2 * F_PAD), BF16),
        compiler_params=_params(("parallel", "parallel")),
        name="expert_gate_up_prep",
    )(w_gu)


def _expert_small_weights(b_gu, w_dn, b_dn):
    e = b_gu.shape[0]
    pad = ((0, 0), (0, F_PAD - F_EXPERT))
    bgu = jnp.concatenate([jnp.pad(b_gu[:, 0::2], pad), jnp.pad(b_gu[:, 1::2], pad)], axis=1).reshape(e, 1, 2 * F_PAD)
    wdn = jnp.pad(w_dn, ((0, 0), (0, F_PAD - F_EXPERT), (0, 0))).astype(BF16)
    return bgu, wdn, b_dn.reshape(e, 1, -1)


def moe_ffn(x, mods, g, router_w, router_b, w_gu, b_gu, w_dn, b_dn, mod_of_tile):
    hp, topi, topw = ffn_prep(x, mods, g, router_w, router_b, mod_of_tile)
    tile_e, n_valid, src_token, slot_of = routing_tables(topi)
    wgu = expert_gate_up_prep(w_gu)
    bgu, wdn, bdn = _expert_small_weights(b_gu, w_dn, b_dn)
    ys = expert_sweep(hp, tile_e, n_valid, src_token, wgu, bgu, wdn, bdn)
    return moe_combine(ys, slot_of, topw)


def _final_kernel(x_ref, y_ref, mod_ref, g_ref, o_ref):
    x = x_ref[...] + mod_ref[0, 5:6, :] * y_ref[...].astype(F32)
    ms = jnp.mean(x * x, axis=-1, keepdims=True)
    o_ref[...] = x * lax.rsqrt(ms + NORM_EPS) * g_ref[...]


def final_norm(x, y, mods, g, n_rows, mod_of_tile):
    d = x.shape[1]
    tm = PREP_TILE
    tiles_per_row_tile = ROW_TILE // tm
    return pl.pallas_call(
        _final_kernel,
        grid=(n_rows // tm,),
        in_specs=[
            pl.BlockSpec((tm, d), lambda i: (i, 0)),
            pl.BlockSpec((tm, d), lambda i: (i, 0)),
            pl.BlockSpec((1, N_MOD, d), lambda i: (mod_of_tile(i // tiles_per_row_tile), 0, 0)),
            pl.BlockSpec((1, d), lambda i: (0, 0)),
        ],
        out_specs=pl.BlockSpec((tm, d), lambda i: (i, 0)),
        out_shape=jax.ShapeDtypeStruct((n_rows, d), F32),
        compiler_params=_params(("parallel",)),
        name="final_norm",
    )(x, y, mods, g.reshape(1, d))


def _rope_tables(seq, ctx_len):
    n_freq = HEAD_DIM // 4
    rows = seq // GRID_W
    row = jnp.repeat(jnp.arange(rows, dtype=F32), GRID_W)
    col = jnp.tile(jnp.arange(GRID_W, dtype=F32), rows)
    inv_freq = ROPE_THETA ** (-jnp.arange(n_freq, dtype=F32) / n_freq)
    ang_r = row[:, None] * inv_freq
    ang_c = col[:, None] * inv_freq
    cos = jnp.concatenate([jnp.cos(ang_r)] * 2 + [jnp.cos(ang_c)] * 2, axis=-1)
    sin = jnp.concatenate([-jnp.sin(ang_r), jnp.sin(ang_r), -jnp.sin(ang_c), jnp.sin(ang_c)], axis=-1)
    cos = jnp.concatenate([jnp.ones((ctx_len, HEAD_DIM), F32), cos], axis=0)
    sin = jnp.concatenate([jnp.zeros((ctx_len, HEAD_DIM), F32), sin], axis=0)
    return cos, sin


def _column_tiles(w):
    k, n = w.shape
    return w.astype(BF16).reshape(k, n // COL_TILE, COL_TILE).transpose(1, 0, 2)


def _pad_lanes(v):
    v = v.reshape(1, -1).astype(F32)
    return jnp.pad(v, ((0, 0), (0, LANES - v.shape[1])))


def kernel(x, c, ctx, c_ctx, ada_w, ada_b, norm_mix_g, norm_ffn_g, router_w, router_b, exp_w_gu, exp_b_gu,
           exp_w_down, exp_b_down, ev_w_in, ev_w_out, ev_q_g, ev_k_g, ev_conv_w, ev_conv_b, ev_a_log,
           ev_dt_bias, ev_d_skip, ev_ssm_g, od_w_in, od_w_out, od_sinks, final_g):
    batch, seq, d = x.shape
    ctx_len = ctx.shape[1]
    n_lat = batch * seq
    m = n_lat + batch * ctx_len
    assert d == D_MODEL and seq % ROW_TILE == 0 and batch * ctx_len == ROW_TILE and ctx_len == ATTN_TQ
    assert batch + 1 <= 8

    xs = jnp.concatenate([x.reshape(n_lat, d), ctx.reshape(batch * ctx_len, d)], axis=0)
    cvecs = jnp.concatenate([c, c_ctx[None, :], jnp.zeros((8 - batch - 1, d), F32)], axis=0)
    mods_all = ada_mods(cvecs, ada_w, ada_b).reshape(ada_w.shape[0], 8, N_MOD, d)[:, :batch + 1]

    tiles_per_batch = seq // ROW_TILE

    def mod_of_tile(i):
        return jnp.minimum(i // tiles_per_batch, batch)

    prep_ctx0 = n_lat // PREP_TILE
    pos_tiles = seq // PREP_TILE

    def pos_of_tile(i):
        return jnp.where(i < prep_ctx0, 1 + i % pos_tiles, 0)

    cos_t, sin_t = _rope_tables(seq, ctx_len)
    scale = HEAD_DIM ** -0.5 * LOG2_E

    prev = None
    for i in range(DEPTH):
        p = i // 2
        mods = mods_all[i]
        if i % 2 == 0:
            w_in = ev_w_in[p]
            w_main = _column_tiles(w_in[:, :EVEN_MAIN])
            w_dt = jnp.pad(w_in[:, EVEN_MAIN:], ((0, 0), (0, LANES - 2 * B_HEADS))).astype(BF16)
            yproj, dt_raw = in_projection(xs, prev, mods, norm_mix_g[i], w_main, w_dt, mod_of_tile)
            gain_q = (jnp.tile(ev_q_g[p], A_HEADS) * scale).reshape(1, -1)
            gain_k = jnp.tile(ev_k_g[p], A_KV_HEADS).reshape(1, -1)
            q_t = head_prep(yproj, 0, A_Q, (gain_q, cos_t, sin_t), pos_of_tile, use_norm=True, transpose=True)
            k = head_prep(yproj, A_Q, A_KV, (gain_k, cos_t, sin_t), pos_of_tile, use_norm=True, transpose=False)
            v_t = head_prep(yproj, EVEN_QK, A_KV, None, pos_of_tile, use_norm=False, transpose=True)
            attn = attention(q_t, k, v_t, None, batch=batch, seq=seq, ctx_len=ctx_len, n_heads=A_HEADS,
                             n_kv=A_KV_HEADS, windowed=False)
            z_col = A_Q + 2 * A_KV
            conv_w = jnp.pad(ev_conv_w[p], ((0, 8 - B_CONV_W), (0, 0)))
            xbc = conv_silu(yproj, conv_w, ev_conv_b[p].reshape(1, -1), col0=z_col + B_D_INNER, seq=seq,
                            n_lat_rows=n_lat)
            y_f, y_b = ssd_scan(xbc, dt_raw, _pad_lanes(ev_dt_bias[p]), _pad_lanes(ev_a_log[p]),
                                batch=batch, seq=seq, ctx_len=ctx_len)
            skip_row = jnp.repeat(ev_d_skip[p, 0] + ev_d_skip[p, 1], B_HEAD_DIM).reshape(1, -1)
            ssm = gated_norm(y_f, y_b, xbc, yproj, skip_row, ev_ssm_g[p].reshape(1, -1), z_col=z_col)
            xs = out_projection([attn, ssm], _column_tiles(ev_w_out[p]), xs, prev, mods, mod_of_tile)
        else:
            yproj, _ = in_projection(xs, prev, mods, norm_mix_g[i], _column_tiles(od_w_in[p]), None, mod_of_tile)
            gain_q = jnp.full((1, C_Q), scale, F32)
            gain_k = jnp.ones((1, C_KV), F32)
            q_t = head_prep(yproj, 0, C_Q, (gain_q, cos_t, sin_t), pos_of_tile, use_norm=False, transpose=True)
            k = head_prep(yproj, C_Q, C_KV, (gain_k, cos_t, sin_t), pos_of_tile, use_norm=False, transpose=False)
            v_t = head_prep(yproj, ODD_QK, C_KV, None, pos_of_tile, use_norm=False, transpose=True)
            attn = attention(q_t, k, v_t, od_sinks[p], batch=batch, seq=seq, ctx_len=ctx_len, n_heads=C_HEADS,
                             n_kv=C_KV_HEADS, windowed=True)
            xs = out_projection([attn], _column_tiles(od_w_out[p]), xs, prev, mods, mod_of_tile)
        rw = jnp.pad(router_w[i], ((0, 0), (0, LANES - N_EXPERTS)))
        y = moe_ffn(xs, mods, norm_ffn_g[i], rw, _pad_lanes(router_b[i]), exp_w_gu[i], exp_b_gu[i],
                    exp_w_down[i], exp_b_down[i], mod_of_tile)
        prev = (y, mods)

    out = final_norm(xs, prev[0], prev[1], final_g, n_lat, mod_of_tile)
    return out.reshape(batch, seq, d)
```

```python
import functools
import math

import jax
import jax.numpy as jnp
import numpy as np
from jax import lax
from jax.experimental import pallas as pl
from jax.experimental.pallas import tpu as pltpu

F32 = jnp.float32
BF16 = jnp.bfloat16

D_MODEL = 4096
HEAD_DIM = 128
GRID_W = 64
ROPE_THETA = 10000.0
NORM_EPS = 1e-6
N_MOD = 6
DEPTH = 4

A_HEADS = D_MODEL // (2 * HEAD_DIM)
A_KV_HEADS = A_HEADS // 4
A_Q = A_HEADS * HEAD_DIM
A_KV = A_KV_HEADS * HEAD_DIM

B_D_INNER = D_MODEL // 2
B_HEAD_DIM = 64
B_HEADS = B_D_INNER // B_HEAD_DIM
B_GROUPS = 4
B_HPG = B_HEADS // B_GROUPS
B_STATE = 128
B_GN = B_GROUPS * B_STATE
B_CONV_CH = B_D_INNER + 2 * B_GN
B_CONV_W = 5
B_CHUNK = 128
B_GROUP_CH = B_D_INNER // B_GROUPS

C_HEADS = D_MODEL // HEAD_DIM
C_KV_HEADS = C_HEADS // 4
C_Q = C_HEADS * HEAD_DIM
C_KV = C_KV_HEADS * HEAD_DIM
C_WINDOW = 128

N_EXPERTS = 32
TOP_K = 4
F_EXPERT = 192
SWIGLU_LIMIT = 7.0
SWIGLU_ALPHA = 1.702
F_PAD = 256
GROUP_TILE = 256
COMBINE_TILE = 128
SORT_KEY_SHIFT = 17
HALF_D = D_MODEL // 2

LANES = 128
ROW_TILE = 512
COL_TILE = 512
ATTN_TQ = 256
ATTN_TK = 512
ATTN_L_ROWS = 16
LOG2_E = math.log2(math.e)
PREP_TILE = 256
VMEM_LIMIT = 56 * 1024 * 1024

EVEN_MAIN = A_Q + 2 * A_KV + B_D_INNER + B_CONV_CH
EVEN_QK = A_Q + A_KV
ODD_IN = C_Q + 2 * C_KV
ODD_QK = C_Q + C_KV


def _params(sem):
    return pltpu.CompilerParams(dimension_semantics=sem, vmem_limit_bytes=VMEM_LIMIT)


def _sigmoid(v):
    return 1.0 / (1.0 + jnp.exp(-v))


def _softplus(v):
    return jnp.maximum(v, 0.0) + jnp.log(1.0 + jnp.exp(-jnp.abs(v)))


def _ada_kernel(c_ref, w_ref, b_ref, o_ref):
    c = c_ref[...]
    s = (c * _sigmoid(c)).astype(BF16)
    o_ref[0] = jnp.dot(s, w_ref[0].astype(BF16), preferred_element_type=F32) + b_ref[0]


def ada_mods(cvecs, ada_w, ada_b):
    depth, d, n = ada_w.shape
    tn = COL_TILE
    return pl.pallas_call(
        _ada_kernel,
        grid=(depth, n // tn),
        in_specs=[
            pl.BlockSpec((8, d), lambda l, j: (0, 0)),
            pl.BlockSpec((1, d, tn), lambda l, j: (l, 0, j)),
            pl.BlockSpec((1, 1, tn), lambda l, j: (l, 0, j)),
        ],
        out_specs=pl.BlockSpec((1, 8, tn), lambda l, j: (l, 0, j)),
        out_shape=jax.ShapeDtypeStruct((depth, 8, n), F32),
        compiler_params=_params(("parallel", "parallel")),
        name="ada_mods",
    )(cvecs, ada_w, ada_b.reshape(depth, 1, n))


def _norm_mod(x, g, shift, scale):
    ms = jnp.mean(x * x, axis=-1, keepdims=True)
    return (x * lax.rsqrt(ms + NORM_EPS) * g) * (1.0 + scale) + shift


def _inproj_kernel(*refs, has_prev, has_dt):
    it = iter(refs)
    x_ref = next(it)
    if has_prev:
        y_ref = next(it)
        pmod_ref = next(it)
    mod_ref = next(it)
    g_ref = next(it)
    w_ref = next(it)
    if has_dt:
        wdt_ref = next(it)
    o_ref = next(it)
    if has_dt:
        dt_ref = next(it)
    h_scr = next(it)

    @pl.when(pl.program_id(1) == 0)
    def _():
        x = x_ref[...]
        if has_prev:
            x = x + pmod_ref[0, 5:6, :] * y_ref[...].astype(F32)
        h = _norm_mod(x, g_ref[...], mod_ref[0, 0:1, :], mod_ref[0, 1:2, :])
        h_scr[...] = h.astype(BF16)
        if has_dt:
            dt_ref[...] = jnp.dot(h_scr[...], wdt_ref[...], preferred_element_type=F32)

    o_ref[...] = jnp.dot(h_scr[...], w_ref[...], preferred_element_type=F32).astype(o_ref.dtype)


def in_projection(x, prev, mods, g, w, w_dt, mod_of_tile):
    m, d = x.shape
    tm, tn = ROW_TILE, COL_TILE
    n = w.shape[0] * tn
    has_prev = prev is not None
    has_dt = w_dt is not None
    mod_spec = pl.BlockSpec((1, N_MOD, d), lambda i, j: (mod_of_tile(i), 0, 0))
    args = [x]
    specs = [pl.BlockSpec((tm, d), lambda i, j: (i, 0))]
    if has_prev:
        args += [prev[0], prev[1]]
        specs += [pl.BlockSpec((tm, d), lambda i, j: (i, 0)), mod_spec]
    args += [mods, g.reshape(1, d), w]
    specs += [mod_spec, pl.BlockSpec((1, d), lambda i, j: (0, 0)),
              pl.BlockSpec((None, d, tn), lambda i, j: (j, 0, 0))]
    out_shape = [jax.ShapeDtypeStruct((m, n), BF16)]
    out_specs = [pl.BlockSpec((tm, tn), lambda i, j: (i, j))]
    if has_dt:
        args.append(w_dt)
        specs.append(pl.BlockSpec((d, LANES), lambda i, j: (0, 0)))
        out_shape.append(jax.ShapeDtypeStruct((m, LANES), F32))
        out_specs.append(pl.BlockSpec((tm, LANES), lambda i, j: (i, 0)))
    res = pl.pallas_call(
        functools.partial(_inproj_kernel, has_prev=has_prev, has_dt=has_dt),
        grid=(m // tm, n // tn),
        in_specs=specs,
        out_specs=out_specs,
        out_shape=out_shape,
        scratch_shapes=[pltpu.VMEM((tm, d), BF16)],
        compiler_params=_params(("parallel", "arbitrary")),
        name="in_projection",
    )(*args)
    return res if has_dt else (res[0], None)


def _head_prep_kernel(*refs, use_norm, use_rope, transpose):
    y_ref = refs[0]
    o_ref = refs[-1]
    if use_rope:
        gain_ref, cos_ref, sin_ref = refs[1:4]
        cos = cos_ref[...]
        sin = sin_ref[...]
        lane = lax.broadcasted_iota(jnp.int32, cos.shape, 1)
        first_half = (lane & (HEAD_DIM // 4)) == 0
    for hh in range(COL_TILE // HEAD_DIM):
        sl = slice(hh * HEAD_DIM, (hh + 1) * HEAD_DIM)
        v = y_ref[:, sl].astype(F32)
        if use_norm:
            v = v * lax.rsqrt(jnp.mean(v * v, axis=-1, keepdims=True) + NORM_EPS)
        if use_rope:
            v = v * gain_ref[:, sl]
            swapped = jnp.where(first_half, pltpu.roll(v, HEAD_DIM - HEAD_DIM // 4, 1),
                                pltpu.roll(v, HEAD_DIM // 4, 1))
            v = v * cos + swapped * sin
        if transpose:
            o_ref[sl, :] = v.T.astype(o_ref.dtype)
        else:
            o_ref[:, sl] = v.astype(o_ref.dtype)


def head_prep(yproj, col0, n_cols, rope, pos_of_tile, *, use_norm, transpose):
    m = yproj.shape[0]
    tm = PREP_TILE
    c0 = col0 // COL_TILE
    use_rope = rope is not None
    args = [yproj]
    specs = [pl.BlockSpec((tm, COL_TILE), lambda i, j: (i, c0 + j))]
    if use_rope:
        args += list(rope)
        specs += [pl.BlockSpec((1, COL_TILE), lambda i, j: (0, j)),
                  pl.BlockSpec((tm, HEAD_DIM), lambda i, j: (pos_of_tile(i), 0)),
                  pl.BlockSpec((tm, HEAD_DIM), lambda i, j: (pos_of_tile(i), 0))]
    if transpose:
        out_spec = pl.BlockSpec((COL_TILE, tm), lambda i, j: (j, i))
        out_shape = jax.ShapeDtypeStruct((n_cols, m), BF16)
    else:
        out_spec = pl.BlockSpec((tm, COL_TILE), lambda i, j: (i, j))
        out_shape = jax.ShapeDtypeStruct((m, n_cols), BF16)
    return pl.pallas_call(
        functools.partial(_head_prep_kernel, use_norm=use_norm, use_rope=use_rope, transpose=transpose),
        grid=(m // tm, n_cols // COL_TILE),
        in_specs=specs,
        out_specs=out_spec,
        out_shape=out_shape,
        compiler_params=_params(("parallel", "parallel")),
        name="head_prep",
    )(*args)


def _attn_kernel(*refs, group, n_lat_tiles, seq, windowed, has_sink):
    it = iter(refs)
    if has_sink:
        sink_ref = next(it)
    qt_ref, kc_ref, vtc_ref, kl_ref, vtl_ref, o_ref, m_scr, acc_scr, s_scr = (next(it) for _ in range(9))
    tq = ATTN_TQ
    kvh = pl.program_id(1)
    qi = pl.program_id(2)
    cols = group * tq

    q_t = jnp.concatenate([qt_ref[g * HEAD_DIM:(g + 1) * HEAD_DIM, :] for g in range(group)], axis=1)

    if has_sink:
        m_scr[...] = jnp.concatenate(
            [jnp.full((1, tq), sink_ref[kvh * group + g] * LOG2_E, F32) for g in range(group)], axis=1)
        acc_scr[...] = jnp.concatenate([jnp.zeros((HEAD_DIM, cols), F32), jnp.ones((ATTN_L_ROWS, cols), F32)], axis=0)
    else:
        m_scr[...] = jnp.full((1, cols), -jnp.inf, F32)
        acc_scr[...] = jnp.zeros((HEAD_DIM + ATTN_L_ROWS, cols), F32)

    def scores(k):
        return jnp.dot(k, q_t, preferred_element_type=F32)

    def softmax_pv(s, v_t):
        m_prev = m_scr[...]
        m_new = jnp.maximum(m_prev, jnp.max(s, axis=0, keepdims=True))
        alpha = jnp.exp2(m_prev - m_new)
        p = jnp.exp2(s - m_new).astype(BF16)
        v_ext = jnp.concatenate([v_t, jnp.ones((ATTN_L_ROWS, v_t.shape[1]), BF16)], axis=0)
        acc_scr[...] = alpha * acc_scr[...] + jnp.dot(v_ext, p, preferred_element_type=F32)
        m_scr[...] = m_new

    @pl.when(qi >= n_lat_tiles)
    def _():
        softmax_pv(scores(kc_ref[...]), vtc_ref[...])

    @pl.when(qi < n_lat_tiles)
    def _():
        s_ctx = scores(kc_ref[...])
        if windowed:
            band = tq + 2 * C_WINDOW
            start = jnp.clip(qi * tq - C_WINDOW, 0, seq - band)
            start = pl.multiple_of(start, C_WINDOW)
            k_pos = start + lax.broadcasted_iota(jnp.int32, (band, cols), 0)
            q_pos = qi * tq + (lax.broadcasted_iota(jnp.int32, (band, cols), 1) & (tq - 1))
            mask = jnp.abs(k_pos - q_pos) <= C_WINDOW
            s_scr[0] = jnp.where(mask, scores(kl_ref[pl.ds(start, band), :]), -jnp.inf)
            softmax_pv(s_ctx, vtc_ref[...])
            softmax_pv(s_scr[0], vtl_ref[:, pl.ds(start, band)])
        else:
            n_chunks = seq // ATTN_TK

            def k_chunk(c):
                return kl_ref[pl.ds(pl.multiple_of(c * ATTN_TK, ATTN_TK), ATTN_TK), :]

            def v_chunk(c):
                return vtl_ref[:, pl.ds(pl.multiple_of(c * ATTN_TK, ATTN_TK), ATTN_TK)]

            s_scr[0] = scores(k_chunk(0))
            softmax_pv(s_ctx, vtc_ref[...])

            def body(c2, carry):
                c = 2 * c2
                s_scr[1] = scores(k_chunk(c + 1))
                softmax_pv(s_scr[0], v_chunk(c))
                s_scr[0] = scores(k_chunk(jnp.minimum(c + 2, n_chunks - 1)))
                softmax_pv(s_scr[1], v_chunk(c + 1))
                return carry
            if n_chunks >= 2:
                lax.fori_loop(0, n_chunks // 2, body, 0)
            if n_chunks % 2:
                softmax_pv(s_scr[0], v_chunk(n_chunks - 1))

    out_t = acc_scr[0:HEAD_DIM, :] / acc_scr[HEAD_DIM:HEAD_DIM + 1, :]
    for g in range(group):
        o_ref[:, g * HEAD_DIM:(g + 1) * HEAD_DIM] = out_t[:, g * tq:(g + 1) * tq].T.astype(o_ref.dtype)


def attention(q_t, k, v_t, sinks, *, batch, seq, ctx_len, n_heads, n_kv, windowed):
    m = k.shape[0]
    group = n_heads // n_kv
    tq = ATTN_TQ
    assert ctx_len == tq and seq % ATTN_TK == 0 and tq + 2 * C_WINDOW == ATTN_TK and tq & (tq - 1) == 0
    n_lat_tiles = seq // tq
    ctx_block0 = batch * seq // ctx_len
    has_sink = sinks is not None

    def q_tile(b, i):
        return jnp.where(i < n_lat_tiles, b * n_lat_tiles + i, batch * n_lat_tiles + b)

    in_specs = [
        pl.BlockSpec((group * HEAD_DIM, tq), lambda b, h, i, *_: (h, q_tile(b, i))),
        pl.BlockSpec((ctx_len, HEAD_DIM), lambda b, h, i, *_: (ctx_block0 + b, h)),
        pl.BlockSpec((HEAD_DIM, ctx_len), lambda b, h, i, *_: (h, ctx_block0 + b)),
        pl.BlockSpec((seq, HEAD_DIM), lambda b, h, i, *_: (b, h)),
        pl.BlockSpec((HEAD_DIM, seq), lambda b, h, i, *_: (h, b)),
    ]
    args = [q_t, k, v_t, k, v_t]
    cols = group * tq
    grid_spec = pltpu.PrefetchScalarGridSpec(
        num_scalar_prefetch=1 if has_sink else 0,
        grid=(batch, n_kv, n_lat_tiles + 1),
        in_specs=in_specs,
        out_specs=pl.BlockSpec((tq, group * HEAD_DIM), lambda b, h, i, *_: (q_tile(b, i), h)),
        scratch_shapes=[pltpu.VMEM((1, cols), F32), pltpu.VMEM((HEAD_DIM + ATTN_L_ROWS, cols), F32),
                        pltpu.VMEM((2, ATTN_TK, cols), F32)],
    )
    if has_sink:
        args = [sinks] + args
    return pl.pallas_call(
        functools.partial(_attn_kernel, group=group, n_lat_tiles=n_lat_tiles, seq=seq,
                          windowed=windowed, has_sink=has_sink),
        grid_spec=grid_spec,
        out_shape=jax.ShapeDtypeStruct((m, n_heads * HEAD_DIM), BF16),
        compiler_params=_params(("parallel", "parallel", "arbitrary")),
        name="attention_window" if windowed else "attention_global",
    )(*args)


CONV_HALO = 16


def _conv_kernel(prev_ref, cur_ref, next_ref, w_ref, b_ref, o_ref, *, tiles_per_seq, n_lat_tiles):
    i = pl.program_id(0)
    tm = cur_ref.shape[0]
    in_lat = i < n_lat_tiles
    pos = i % tiles_per_seq
    has_prev = jnp.logical_and(in_lat, pos != 0)
    has_next = jnp.logical_and(in_lat, pos != tiles_per_seq - 1)
    prev = jnp.where(has_prev, prev_ref[...].astype(F32), 0.0)
    nxt = jnp.where(has_next, next_ref[...].astype(F32), 0.0)
    ext = jnp.concatenate([prev, cur_ref[...].astype(F32), nxt], axis=0)
    n_ext = tm + 2 * CONV_HALO
    acc = jnp.zeros((tm, ext.shape[1]), F32) + b_ref[...]
    for k in range(B_CONV_W):
        shift = (B_CONV_W // 2 - k) % n_ext
        shifted = ext if shift == 0 else pltpu.roll(ext, shift, 0)
        acc = acc + shifted[CONV_HALO:CONV_HALO + tm, :] * w_ref[k:k + 1, :]
    o_ref[...] = (acc * _sigmoid(acc)).astype(o_ref.dtype)


def conv_silu(yproj, conv_w, conv_b, *, col0, seq, n_lat_rows):
    m = yproj.shape[0]
    tm = PREP_TILE
    tiles_per_seq = seq // tm
    n_lat_tiles = n_lat_rows // tm
    c0 = col0 // COL_TILE
    halo_per_tile = tm // CONV_HALO
    n_halo_blocks = m // CONV_HALO
    return pl.pallas_call(
        functools.partial(_conv_kernel, tiles_per_seq=tiles_per_seq, n_lat_tiles=n_lat_tiles),
        grid=(m // tm, B_CONV_CH // COL_TILE),
        in_specs=[
            pl.BlockSpec((CONV_HALO, COL_TILE), lambda i, j: (jnp.maximum(i * halo_per_tile - 1, 0), c0 + j)),
            pl.BlockSpec((tm, COL_TILE), lambda i, j: (i, c0 + j)),
            pl.BlockSpec((CONV_HALO, COL_TILE),
                         lambda i, j: (jnp.minimum((i + 1) * halo_per_tile, n_halo_blocks - 1), c0 + j)),
            pl.BlockSpec((8, COL_TILE), lambda i, j: (0, j)),
            pl.BlockSpec((1, COL_TILE), lambda i, j: (0, j)),
        ],
        out_specs=pl.BlockSpec((tm, COL_TILE), lambda i, j: (i, j)),
        out_shape=jax.ShapeDtypeStruct((m, B_CONV_CH), BF16),
        compiler_params=_params(("parallel", "parallel")),
        name="conv_silu",
    )(yproj, yproj, yproj, conv_w, conv_b)


def _ssd_kernel(xf_ref, bf_ref, cf_ref, dtf_ref, xb_ref, bb_ref, cb_ref, dtb_ref, bias_ref, alog_ref,
                yf_ref, yb_ref, h_scr):
    t = pl.program_id(1)
    q = B_CHUNK

    @pl.when(t == 0)
    def _():
        h_scr[...] = jnp.zeros(h_scr.shape, F32)

    row = lax.broadcasted_iota(jnp.int32, (q, q), 0)
    col = lax.broadcasted_iota(jnp.int32, (q, q), 1)
    lane_lo = lax.broadcasted_iota(jnp.int32, (q, LANES), 1) < B_HEAD_DIM
    lane_lo_row = lane_lo[0:1, :]
    a_coef = -jnp.exp(alog_ref[...])

    dirs = ((xf_ref, bf_ref, cf_ref, dtf_ref, yf_ref, col <= row, q - 1),
            (xb_ref, bb_ref, cb_ref, dtb_ref, yb_ref, col >= row, 0))
    for d, (x_ref, b_ref, c_ref, dt_ref, y_ref, tri, last_row) in enumerate(dirs):
        dt = _softplus(dt_ref[...] + bias_ref[...])
        ld = dt * a_coef
        cum = jnp.dot(tri.astype(F32), ld, preferred_element_type=F32, precision=lax.Precision.HIGHEST)
        cum_t = cum.T
        last = cum[last_row:last_row + 1, :]
        e_cum = jnp.exp(cum)
        e_rest = jnp.exp(last - cum)
        e_last = jnp.exp(last)

        def pick(arr, l0, lo=lane_lo):
            return jnp.where(lo, arr[:, l0:l0 + 1], arr[:, l0 + 1:l0 + 2])

        for g in range(B_GROUPS):
            bm = b_ref[:, g * B_STATE:(g + 1) * B_STATE]
            cm = c_ref[:, g * B_STATE:(g + 1) * B_STATE]
            cbm = lax.dot_general(cm, bm, (((1,), (1,)), ((), ())), preferred_element_type=F32)
            bm_t = bm.astype(F32).T.astype(BF16)
            h_prev = h_scr[d, g]
            y_off = jnp.dot(cm, h_prev.astype(BF16), preferred_element_type=F32)
            xw_parts = []
            dec_parts = []
            for p in range(B_HPG // 2):
                l0 = d * (B_GROUPS * B_HPG) + g * B_HPG + 2 * p
                c0 = g * B_GROUP_CH + p * LANES
                xdt = x_ref[:, c0:c0 + LANES].astype(F32) * pick(dt, l0)
                y_p = y_off[:, p * LANES:(p + 1) * LANES] * pick(e_cum, l0)
                decay_pair = []
                x_pair = []
                for half in range(2):
                    l = l0 + half
                    seg = jnp.exp(jnp.where(tri, cum[:, l:l + 1] - cum_t[l:l + 1, :], -jnp.inf))
                    keep = lane_lo if half == 0 else jnp.logical_not(lane_lo)
                    decay_pair.append((cbm * seg).astype(BF16))
                    x_pair.append(jnp.where(keep, xdt, 0.0).astype(BF16))
                y_p = y_p + jnp.dot(jnp.concatenate(decay_pair, axis=1), jnp.concatenate(x_pair, axis=0),
                                    preferred_element_type=F32)
                y_ref[:, c0:c0 + LANES] = y_p.astype(y_ref.dtype)
                xw_parts.append((xdt * pick(e_rest, l0)).astype(BF16))
                dec_parts.append(pick(e_last, l0, lane_lo_row))
            xw = jnp.concatenate(xw_parts, axis=1)
            dec = jnp.concatenate(dec_parts, axis=1)
            h_scr[d, g] = h_prev * dec + jnp.dot(bm_t, xw, preferred_element_type=F32)


def ssd_scan(xbc, dt_raw, dt_bias_row, a_log_row, *, batch, seq, ctx_len):
    m = xbc.shape[0]
    q = B_CHUNK
    nc_lat, nc_ctx = seq // q, ctx_len // q
    n_steps = nc_ctx + nc_lat
    ctx0 = batch * nc_lat

    def fwd_blk(b, t):
        return jnp.where(t < nc_ctx, ctx0 + b * nc_ctx + t, b * nc_lat + t - nc_ctx)

    def bwd_blk(b, t):
        return jnp.where(t < nc_ctx, ctx0 + b * nc_ctx + (nc_ctx - 1 - t), b * nc_lat + (n_steps - 1 - t))

    xb, bb, cb = 0, B_D_INNER // B_GN, B_D_INNER // B_GN + 1

    def specs(blk):
        return [
            pl.BlockSpec((q, B_D_INNER), lambda b, t: (blk(b, t), xb)),
            pl.BlockSpec((q, B_GN), lambda b, t: (blk(b, t), bb)),
            pl.BlockSpec((q, B_GN), lambda b, t: (blk(b, t), cb)),
            pl.BlockSpec((q, LANES), lambda b, t: (blk(b, t), 0)),
        ]

    row_spec = pl.BlockSpec((1, LANES), lambda b, t: (0, 0))
    return pl.pallas_call(
        _ssd_kernel,
        grid=(batch, n_steps),
        in_specs=specs(fwd_blk) + specs(bwd_blk) + [row_spec, row_spec],
        out_specs=[pl.BlockSpec((q, B_D_INNER), lambda b, t: (fwd_blk(b, t), 0)),
                   pl.BlockSpec((q, B_D_INNER), lambda b, t: (bwd_blk(b, t), 0))],
        out_shape=[jax.ShapeDtypeStruct((m, B_D_INNER), BF16)] * 2,
        scratch_shapes=[pltpu.VMEM((2, B_GROUPS, B_STATE, B_GROUP_CH), F32)],
        compiler_params=_params(("parallel", "arbitrary")),
        name="ssd_scan",
    )(xbc, xbc, xbc, dt_raw, xbc, xbc, xbc, dt_raw, dt_bias_row, a_log_row)


def _gated_norm_kernel(yf_ref, yb_ref, x_ref, z_ref, skip_ref, g_ref, o_ref):
    z = z_ref[...].astype(F32)
    y = yf_ref[...].astype(F32) + yb_ref[...].astype(F32) + skip_ref[...] * x_ref[...].astype(F32)
    yz = y * (z * _sigmoid(z))
    ms = jnp.mean(yz * yz, axis=-1, keepdims=True)
    o_ref[...] = (yz * lax.rsqrt(ms + NORM_EPS) * g_ref[...]).astype(o_ref.dtype)


def gated_norm(y_f, y_b, xbc, yproj, skip_row, g_row, *, z_col):
    m = y_f.shape[0]
    tm = PREP_TILE
    w = B_GROUP_CH
    z0 = z_col // w
    blk = pl.BlockSpec((tm, w), lambda i, j: (i, j))
    row = pl.BlockSpec((1, w), lambda i, j: (0, j))
    return pl.pallas_call(
        _gated_norm_kernel,
        grid=(m // tm, B_GROUPS),
        in_specs=[blk, blk, blk, pl.BlockSpec((tm, w), lambda i, j: (i, z0 + j)), row, row],
        out_specs=blk,
        out_shape=jax.ShapeDtypeStruct((m, B_D_INNER), BF16),
        compiler_params=_params(("parallel", "parallel")),
        name="gated_norm",
    )(y_f, y_b, xbc, yproj, skip_row, g_row)


def _outproj_kernel(*refs, n_lhs, has_prev):
    it = iter(refs)
    lhs = [next(it) for _ in range(n_lhs)]
    ws = [next(it) for _ in range(n_lhs)]
    x_ref = next(it)
    if has_prev:
        y_ref = next(it)
        pmod_ref = next(it)
    mod_ref = next(it)
    o_ref = next(it)
    acc = jnp.dot(lhs[0][...], ws[0][...], preferred_element_type=F32)
    for a, w in zip(lhs[1:], ws[1:]):
        acc = acc + jnp.dot(a[...], w[...], preferred_element_type=F32)
    x = x_ref[...]
    if has_prev:
        x = x + pmod_ref[0, 5:6, :] * y_ref[...].astype(F32)
    o_ref[...] = x + mod_ref[0, 2:3, :] * acc


def out_projection(lhs, w, x, prev, mods, mod_of_tile):
    m, d = x.shape
    tm, tn = ROW_TILE, COL_TILE
    kw = lhs[0].shape[1]
    has_prev = prev is not None
    mod_spec = pl.BlockSpec((1, N_MOD, tn), lambda i, j: (mod_of_tile(i), 0, j))
    specs = [pl.BlockSpec((tm, kw), lambda i, j: (i, 0)) for _ in lhs]
    specs += [pl.BlockSpec((None, kw, tn), functools.partial(lambda i, j, r: (j, r, 0), r=r))
              for r in range(len(lhs))]
    args = list(lhs) + [w] * len(lhs) + [x]
    specs.append(pl.BlockSpec((tm, tn), lambda i, j: (i, j)))
    if has_prev:
        args += [prev[0], prev[1]]
        specs += [pl.BlockSpec((tm, tn), lambda i, j: (i, j)), mod_spec]
    args.append(mods)
    specs.append(mod_spec)
    return pl.pallas_call(
        functools.partial(_outproj_kernel, n_lhs=len(lhs), has_prev=has_prev),
        grid=(m // tm, d // tn),
        in_specs=specs,
        out_specs=pl.BlockSpec((tm, tn), lambda i, j: (i, j)),
        out_shape=jax.ShapeDtypeStruct((m, d), F32),
        compiler_params=_params(("parallel", "parallel")),
        name="out_projection",
    )(*args)


def _gather_params(sem):
    return pltpu.CompilerParams(dimension_semantics=sem, vmem_limit_bytes=VMEM_LIMIT, disable_bounds_checks=True)


def _pack_bf16_pairs(v):
    bits = lax.bitcast_convert_type(v.astype(BF16).astype(F32), jnp.uint32)
    return bits[:, HALF_D:] | (bits[:, :HALF_D] >> 16)


def _unpack_bf16_pairs(w):
    lo = lax.bitcast_convert_type(w << 16, F32)
    hi = lax.bitcast_convert_type(w & jnp.uint32(0xFFFF0000), F32)
    return lo, hi


def _ffn_prep_kernel(x_ref, mod_ref, g_ref, rw_ref, rb_ref, hp_ref, topi_ref, topw_ref):
    h = _norm_mod(x_ref[...], g_ref[...], mod_ref[0, 3:4, :], mod_ref[0, 4:5, :])
    hp_ref[...] = _pack_bf16_pairs(h)
    logits = jnp.dot(h, rw_ref[...], preferred_element_type=F32, precision=lax.Precision.HIGHEST) + rb_ref[...]
    lane = lax.broadcasted_iota(jnp.int32, logits.shape, 1)
    work = jnp.where(lane < N_EXPERTS, logits, -jnp.inf)
    top = None
    denom = jnp.zeros((logits.shape[0], 1), F32)
    topi = jnp.zeros(logits.shape, jnp.int32)
    topw = jnp.zeros(logits.shape, F32)
    for k in range(TOP_K):
        mx = jnp.max(work, axis=-1, keepdims=True)
        idx = jnp.min(jnp.where(work == mx, lane, LANES), axis=-1, keepdims=True)
        if top is None:
            top = mx
        e = jnp.exp(mx - top)
        denom = denom + e
        topi = jnp.where(lane == k, idx, topi)
        topw = jnp.where(lane == k, e, topw)
        work = jnp.where(lane == idx, -jnp.inf, work)
    topi_ref[...] = topi
    topw_ref[...] = topw / denom


def ffn_prep(x, mods, g, router_w, router_b, mod_of_tile):
    m, d = x.shape
    tm = PREP_TILE
    tiles_per_row_tile = ROW_TILE // tm
    return pl.pallas_call(
        _ffn_prep_kernel,
        grid=(m // tm,),
        in_specs=[
            pl.BlockSpec((tm, d), lambda i: (i, 0)),
            pl.BlockSpec((1, N_MOD, d), lambda i: (mod_of_tile(i // tiles_per_row_tile), 0, 0)),
            pl.BlockSpec((1, d), lambda i: (0, 0)),
            pl.BlockSpec((d, LANES), lambda i: (0, 0)),
            pl.BlockSpec((1, LANES), lambda i: (0, 0)),
        ],
        out_specs=[pl.BlockSpec((tm, d // 2), lambda i: (i, 0)),
                   pl.BlockSpec((tm, LANES), lambda i: (i, 0)),
                   pl.BlockSpec((tm, LANES), lambda i: (i, 0))],
        out_shape=[jax.ShapeDtypeStruct((m, d // 2), jnp.uint32),
                   jax.ShapeDtypeStruct((m, LANES), jnp.int32),
                   jax.ShapeDtypeStruct((m, LANES), F32)],
        compiler_params=_params(("parallel",)),
        name="ffn_prep",
    )(x, mods, g.reshape(1, d), router_w, router_b)


def routing_tables(topi):
    m = topi.shape[0]
    ids = topi[:, :TOP_K].reshape(-1)
    n_assign = ids.shape[0]
    tg = GROUP_TILE
    assert n_assign < (1 << SORT_KEY_SHIFT) and n_assign % tg == 0 and m % COMBINE_TILE == 0
    n_tiles = (n_assign + N_EXPERTS * tg) // tg
    counts = jnp.sum((ids[:, None] == jnp.arange(N_EXPERTS, dtype=jnp.int32)[None, :]).astype(jnp.int32), axis=0)
    padded = ((counts + tg - 1) // tg) * tg
    g_end = jnp.cumsum(padded)
    g_start = g_end - padded
    u_start = jnp.cumsum(counts) - counts
    position = jnp.arange(n_assign, dtype=jnp.int32)
    keys = jnp.sort(ids * (1 << SORT_KEY_SHIFT) + position)
    order = keys & ((1 << SORT_KEY_SHIFT) - 1)
    e_sorted = keys >> SORT_KEY_SHIFT
    slot_sorted = g_start[e_sorted] + position - u_start[e_sorted]
    _, slot = lax.sort((order, slot_sorted), num_keys=1)
    tile_start = jnp.arange(n_tiles, dtype=jnp.int32) * tg
    tile_e = jnp.sum((tile_start[:, None] >= g_end[None, :]).astype(jnp.int32), axis=1)
    n_valid = (g_end[-1] // tg).reshape(1)
    tile_ec = jnp.minimum(tile_e, N_EXPERTS - 1)
    s = jnp.arange(n_tiles * tg, dtype=jnp.int32)
    e_s = jnp.repeat(tile_ec, tg)
    off = s - g_start[e_s]
    valid = jnp.logical_and(off < counts[e_s], jnp.repeat(tile_e, tg) < N_EXPERTS)
    r = jnp.clip(u_start[e_s] + off, 0, n_assign - 1)
    src_token = jnp.where(valid, order[r] // TOP_K, 0).reshape(n_tiles, 1, tg)
    tc = COMBINE_TILE
    slot_of = slot.reshape(m // tc, tc, TOP_K).transpose(0, 2, 1).reshape(m // tc, 1, TOP_K * tc)
    return tile_ec, n_valid, src_token, slot_of


def _row_copy(src_hbm, src_row, dst, dst_row, sem):
    return pltpu.make_async_copy(src_hbm.at[pl.ds(src_row, 1)], dst.at[pl.ds(dst_row, 1)], sem)


def _start_row_gather(idx_ref, n_rows, src_hbm, dst, sem):
    def body(r, carry):
        _row_copy(src_hbm, idx_ref[0, 0, r], dst, r, sem).start()
        return carry
    lax.fori_loop(0, n_rows, body, 0, unroll=8)


def _start_row_gather_inline(idx_ref, n_rows, src_hbm, dst, sem):
    for r in range(n_rows):
        _row_copy(src_hbm, idx_ref[0, 0, r], dst, r, sem).start()


def _wait_row_gather(n_rows, src_hbm, dst, sem):
    def body(r, carry):
        _row_copy(src_hbm, 0, dst, r, sem).wait()
        return carry
    lax.fori_loop(0, n_rows, body, 0, unroll=8)


def _expert_kernel(tile_e_ref, n_valid_ref, idx_ref, idx_next_ref, h_hbm, wgu_ref, bgu_ref, wdn_ref, bdn_ref,
                   o_ref, hbuf, sem):
    t = pl.program_id(0)
    n_valid = n_valid_ref[0]
    cur = t % 2

    @pl.when(t == 0)
    def _():
        _start_row_gather(idx_ref, GROUP_TILE, h_hbm, hbuf.at[0], sem.at[0])

    @pl.when(t < n_valid)
    def _():
        _wait_row_gather(GROUP_TILE, h_hbm, hbuf.at[cur], sem.at[cur])
        _start_row_gather_inline(idx_next_ref, GROUP_TILE, h_hbm, hbuf.at[1 - cur], sem.at[1 - cur])
        lo, hi = _unpack_bf16_pairs(hbuf[cur])
        h = jnp.concatenate([lo.astype(BF16), hi.astype(BF16)], axis=1)
        gu = jnp.dot(h, wgu_ref[0], preferred_element_type=F32) + bgu_ref[0]
        glu = jnp.minimum(gu[:, :F_PAD], SWIGLU_LIMIT)
        lin = jnp.clip(gu[:, F_PAD:], -SWIGLU_LIMIT, SWIGLU_LIMIT)
        act = glu * _sigmoid(SWIGLU_ALPHA * glu) * (lin + 1.0)
        y = jnp.dot(act.astype(BF16), wdn_ref[0], preferred_element_type=F32) + bdn_ref[0]
        o_ref[...] = _pack_bf16_pairs(y)

    @pl.when(t == n_valid)
    def _():
        _wait_row_gather(GROUP_TILE, h_hbm, hbuf.at[cur], sem.at[cur])

    @pl.when(t >= n_valid)
    def _():
        o_ref[...] = jnp.zeros(o_ref.shape, o_ref.dtype)


def expert_sweep(h_packed, tile_e, n_valid, src_token, wgu, bgu, wdn, bdn):
    n_tiles = src_token.shape[0]
    tg = GROUP_TILE
    d = D_MODEL
    grid_spec = pltpu.PrefetchScalarGridSpec(
        num_scalar_prefetch=2,
        grid=(n_tiles,),
        in_specs=[
            pl.BlockSpec((1, 1, tg), lambda t, te, nv: (t, 0, 0), memory_space=pltpu.SMEM),
            pl.BlockSpec((1, 1, tg), lambda t, te, nv: (jnp.minimum(t + 1, n_tiles - 1), 0, 0),
                         memory_space=pltpu.SMEM),
            pl.BlockSpec(memory_space=pl.ANY),
            pl.BlockSpec((1, d, 2 * F_PAD), lambda t, te, nv: (te[t], 0, 0)),
            pl.BlockSpec((1, 1, 2 * F_PAD), lambda t, te, nv: (te[t], 0, 0)),
            pl.BlockSpec((1, F_PAD, d), lambda t, te, nv: (te[t], 0, 0)),
            pl.BlockSpec((1, 1, d), lambda t, te, nv: (te[t], 0, 0)),
        ],
        out_specs=pl.BlockSpec((tg, d // 2), lambda t, te, nv: (t, 0)),
        scratch_shapes=[pltpu.VMEM((2, tg, d // 2), jnp.uint32), pltpu.SemaphoreType.DMA((2,))],
    )
    return pl.pallas_call(
        _expert_kernel,
        grid_spec=grid_spec,
        out_shape=jax.ShapeDtypeStruct((n_tiles * tg, d // 2), jnp.uint32),
        compiler_params=_gather_params(("arbitrary",)),
        name="expert_sweep",
    )(tile_e, n_valid, src_token, src_token, h_packed, wgu, bgu, wdn, bdn)


def _combine_kernel(slot_ref, slot_next_ref, ys_hbm, topw_ref, o_ref, buf, sem):
    i = pl.program_id(0)
    n = pl.num_programs(0)
    cur = i % 2
    n_rows = TOP_K * COMBINE_TILE

    @pl.when(i == 0)
    def _():
        _start_row_gather(slot_ref, n_rows, ys_hbm, buf.at[0], sem.at[0])

    _wait_row_gather(n_rows, ys_hbm, buf.at[cur], sem.at[cur])
    _start_row_gather_inline(slot_next_ref, n_rows, ys_hbm, buf.at[1 - cur], sem.at[1 - cur])
    w = topw_ref[...]
    acc_lo = jnp.zeros((COMBINE_TILE, HALF_D), F32)
    acc_hi = jnp.zeros((COMBINE_TILE, HALF_D), F32)
    for k in range(TOP_K):
        lo, hi = _unpack_bf16_pairs(buf[cur, k * COMBINE_TILE:(k + 1) * COMBINE_TILE, :])
        acc_lo = acc_lo + w[:, k:k + 1] * lo
        acc_hi = acc_hi + w[:, k:k + 1] * hi
    o_ref[:, :HALF_D] = acc_lo.astype(o_ref.dtype)
    o_ref[:, HALF_D:] = acc_hi.astype(o_ref.dtype)

    @pl.when(i == n - 1)
    def _():
        _wait_row_gather(n_rows, ys_hbm, buf.at[1 - cur], sem.at[1 - cur])


def moe_combine(ys_packed, slot_of, topw):
    m = topw.shape[0]
    tc = COMBINE_TILE
    n = m // tc
    d = D_MODEL
    return pl.pallas_call(
        _combine_kernel,
        grid=(n,),
        in_specs=[
            pl.BlockSpec((1, 1, TOP_K * tc), lambda i: (i, 0, 0), memory_space=pltpu.SMEM),
            pl.BlockSpec((1, 1, TOP_K * tc), lambda i: (jnp.minimum(i + 1, n - 1), 0, 0), memory_space=pltpu.SMEM),
            pl.BlockSpec(memory_space=pl.ANY),
            pl.BlockSpec((tc, LANES), lambda i: (i, 0)),
        ],
        out_specs=pl.BlockSpec((tc, d), lambda i: (i, 0)),
        out_shape=jax.ShapeDtypeStruct((m, d), BF16),
        scratch_shapes=[pltpu.VMEM((2, TOP_K * tc, d // 2), jnp.uint32), pltpu.SemaphoreType.DMA((2,))],
        compiler_params=_gather_params(("arbitrary",)),
        name="moe_combine",
    )(slot_of, slot_of, ys_packed, topw)


def _expert_prep_kernel(w_ref, o_ref):
    n_in = 2 * F_EXPERT
    n_out = 2 * F_PAD
    src = lax.broadcasted_iota(jnp.int32, (n_in, n_out), 0)
    dst = lax.broadcasted_iota(jnp.int32, (n_in, n_out), 1)
    perm = jnp.where(dst == (src & 1) * F_PAD + (src >> 1), 1.0, 0.0).astype(BF16)
    o_ref[0] = jnp.dot(w_ref[0].astype(BF16), perm, preferred_element_type=F32).astype(o_ref.dtype)


def expert_gate_up_prep(w_gu):
    e, d, n_in = w_gu.shape
    tk = 1024
    return pl.pallas_call(
        _expert_prep_kernel,
        grid=(e, d // tk),
        in_specs=[pl.BlockSpec((1, tk, n_in), lambda s, k: (s, k, 0))],
        out_specs=pl.BlockSpec((1, tk, 2 * F_PAD), lambda s, k: (s, k, 0)),
        out_shape=jax.ShapeDtypeStruct((e, d, 2 * F_PAD), BF16),
        compiler_params=_params(("parallel", "parallel")),
        name="expert_gate_up_prep",
    )(w_gu)


def _expert_small_weights(b_gu, w_dn, b_dn):
    e = b_gu.shape[0]
    pad = ((0, 0), (0, F_PAD - F_EXPERT))
    bgu = jnp.concatenate([jnp.pad(b_gu[:, 0::2], pad), jnp.pad(b_gu[:, 1::2], pad)], axis=1).reshape(e, 1, 2 * F_PAD)
    wdn = jnp.pad(w_dn, ((0, 0), (0, F_PAD - F_EXPERT), (0, 0))).astype(BF16)
    return bgu, wdn, b_dn.reshape(e, 1, -1)


def moe_ffn(x, mods, g, router_w, router_b, w_gu, b_gu, w_dn, b_dn, mod_of_tile):
    hp, topi, topw = ffn_prep(x, mods, g, router_w, router_b, mod_of_tile)
    tile_e, n_valid, src_token, slot_of = routing_tables(topi)
    wgu = expert_gate_up_prep(w_gu)
    bgu, wdn, bdn = _expert_small_weights(b_gu, w_dn, b_dn)
    ys = expert_sweep(hp, tile_e, n_valid, src_token, wgu, bgu, wdn, bdn)
    return moe_combine(ys, slot_of, topw)


def _final_kernel(x_ref, y_ref, mod_ref, g_ref, o_ref):
    x = x_ref[...] + mod_ref[0, 5:6, :] * y_ref[...].astype(F32)
    ms = jnp.mean(x * x, axis=-1, keepdims=True)
    o_ref[...] = x * lax.rsqrt(ms + NORM_EPS) * g_ref[...]


def final_norm(x, y, mods, g, n_rows, mod_of_tile):
    d = x.shape[1]
    tm = PREP_TILE
    tiles_per_row_tile = ROW_TILE // tm
    return pl.pallas_call(
        _final_kernel,
        grid=(n_rows // tm,),
        in_specs=[
            pl.BlockSpec((tm, d), lambda i: (i, 0)),
            pl.BlockSpec((tm, d), lambda i: (i, 0)),
            pl.BlockSpec((1, N_MOD, d), lambda i: (mod_of_tile(i // tiles_per_row_tile), 0, 0)),
            pl.BlockSpec((1, d), lambda i: (0, 0)),
        ],
        out_specs=pl.BlockSpec((tm, d), lambda i: (i, 0)),
        out_shape=jax.ShapeDtypeStruct((n_rows, d), F32),
        compiler_params=_params(("parallel",)),
        name="final_norm",
    )(x, y, mods, g.reshape(1, d))


def _rope_tables(seq, ctx_len):
    n_freq = HEAD_DIM // 4
    rows = seq // GRID_W
    row = jnp.repeat(jnp.arange(rows, dtype=F32), GRID_W)
    col = jnp.tile(jnp.arange(GRID_W, dtype=F32), rows)
    inv_freq = ROPE_THETA ** (-jnp.arange(n_freq, dtype=F32) / n_freq)
    ang_r = row[:, None] * inv_freq
    ang_c = col[:, None] * inv_freq
    cos = jnp.concatenate([jnp.cos(ang_r)] * 2 + [jnp.cos(ang_c)] * 2, axis=-1)
    sin = jnp.concatenate([-jnp.sin(ang_r), jnp.sin(ang_r), -jnp.sin(ang_c), jnp.sin(ang_c)], axis=-1)
    cos = jnp.concatenate([jnp.ones((ctx_len, HEAD_DIM), F32), cos], axis=0)
    sin = jnp.concatenate([jnp.zeros((ctx_len, HEAD_DIM), F32), sin], axis=0)
    return cos, sin


def _column_tiles(w):
    k, n = w.shape
    return w.astype(BF16).reshape(k, n // COL_TILE, COL_TILE).transpose(1, 0, 2)


def _pad_lanes(v):
    v = v.reshape(1, -1).astype(F32)
    return jnp.pad(v, ((0, 0), (0, LANES - v.shape[1])))


def kernel(x, c, ctx, c_ctx, ada_w, ada_b, norm_mix_g, norm_ffn_g, router_w, router_b, exp_w_gu, exp_b_gu,
           exp_w_down, exp_b_down, ev_w_in, ev_w_out, ev_q_g, ev_k_g, ev_conv_w, ev_conv_b, ev_a_log,
           ev_dt_bias, ev_d_skip, ev_ssm_g, od_w_in, od_w_out, od_sinks, final_g):
    batch, seq, d = x.shape
    ctx_len = ctx.shape[1]
    n_lat = batch * seq
    m = n_lat + batch * ctx_len
    assert d == D_MODEL and seq % ROW_TILE == 0 and batch * ctx_len == ROW_TILE and ctx_len == ATTN_TQ
    assert batch + 1 <= 8

    xs = jnp.concatenate([x.reshape(n_lat, d), ctx.reshape(batch * ctx_len, d)], axis=0)
    cvecs = jnp.concatenate([c, c_ctx[None, :], jnp.zeros((8 - batch - 1, d), F32)], axis=0)
    mods_all = ada_mods(cvecs, ada_w, ada_b).reshape(ada_w.shape[0], 8, N_MOD, d)[:, :batch + 1]

    tiles_per_batch = seq // ROW_TILE

    def mod_of_tile(i):
        return jnp.minimum(i // tiles_per_batch, batch)

    prep_ctx0 = n_lat // PREP_TILE
    pos_tiles = seq // PREP_TILE

    def pos_of_tile(i):
        return jnp.where(i < prep_ctx0, 1 + i % pos_tiles, 0)

    cos_t, sin_t = _rope_tables(seq, ctx_len)
    scale = HEAD_DIM ** -0.5 * LOG2_E

    prev = None
    for i in range(DEPTH):
        p = i // 2
        mods = mods_all[i]
        if i % 2 == 0:
            w_in = ev_w_in[p]
            w_main = _column_tiles(w_in[:, :EVEN_MAIN])
            w_dt = jnp.pad(w_in[:, EVEN_MAIN:], ((0, 0), (0, LANES - 2 * B_HEADS))).astype(BF16)
            yproj, dt_raw = in_projection(xs, prev, mods, norm_mix_g[i], w_main, w_dt, mod_of_tile)
            gain_q = (jnp.tile(ev_q_g[p], A_HEADS) * scale).reshape(1, -1)
            gain_k = jnp.tile(ev_k_g[p], A_KV_HEADS).reshape(1, -1)
            q_t = head_prep(yproj, 0, A_Q, (gain_q, cos_t, sin_t), pos_of_tile, use_norm=True, transpose=True)
            k = head_prep(yproj, A_Q, A_KV, (gain_k, cos_t, sin_t), pos_of_tile, use_norm=True, transpose=False)
            v_t = head_prep(yproj, EVEN_QK, A_KV, None, pos_of_tile, use_norm=False, transpose=True)
            attn = attention(q_t, k, v_t, None, batch=batch, seq=seq, ctx_len=ctx_len, n_heads=A_HEADS,
                             n_kv=A_KV_HEADS, windowed=False)
            z_col = A_Q + 2 * A_KV
            conv_w = jnp.pad(ev_conv_w[p], ((0, 8 - B_CONV_W), (0, 0)))
            xbc = conv_silu(yproj, conv_w, ev_conv_b[p].reshape(1, -1), col0=z_col + B_D_INNER, seq=seq,
                            n_lat_rows=n_lat)
            y_f, y_b = ssd_scan(xbc, dt_raw, _pad_lanes(ev_dt_bias[p]), _pad_lanes(ev_a_log[p]),
                                batch=batch, seq=seq, ctx_len=ctx_len)
            skip_row = jnp.repeat(ev_d_skip[p, 0] + ev_d_skip[p, 1], B_HEAD_DIM).reshape(1, -1)
            ssm = gated_norm(y_f, y_b, xbc, yproj, skip_row, ev_ssm_g[p].reshape(1, -1), z_col=z_col)
            xs = out_projection([attn, ssm], _column_tiles(ev_w_out[p]), xs, prev, mods, mod_of_tile)
        else:
            yproj, _ = in_projection(xs, prev, mods, norm_mix_g[i], _column_tiles(od_w_in[p]), None, mod_of_tile)
            gain_q = jnp.full((1, C_Q), scale, F32)
            gain_k = jnp.ones((1, C_KV), F32)
            q_t = head_prep(yproj, 0, C_Q, (gain_q, cos_t, sin_t), pos_of_tile, use_norm=False, transpose=True)
            k = head_prep(yproj, C_Q, C_KV, (gain_k, cos_t, sin_t), pos_of_tile, use_norm=False, transpose=False)
            v_t = head_prep(yproj, ODD_QK, C_KV, None, pos_of_tile, use_norm=False, transpose=True)
            attn = attention(q_t, k, v_t, od_sinks[p], batch=batch, seq=seq, ctx_len=ctx_len, n_heads=C_HEADS,
                             n_kv=C_KV_HEADS, windowed=True)
            xs = out_projection([attn], _column_tiles(od_w_out[p]), xs, prev, mods, mod_of_tile)
        rw = jnp.pad(router_w[i], ((0, 0), (0, LANES - N_EXPERTS)))
        y = moe_ffn(xs, mods, norm_ffn_g[i], rw, _pad_lanes(router_b[i]), exp_w_gu[i], exp_b_gu[i],
                    exp_w_down[i], exp_b_down[i], mod_of_tile)
        prev = (y, mods)

    out = final_norm(xs, prev[0], prev[1], final_g, n_lat, mod_of_tile)
    return out.reshape(batch, seq, d)
```
